```python
import jax, jax.numpy as jnp
from jax import lax
import numpy as np

D_MODEL = 2048
BATCH = 4
SEQ = 2048
DEPTH = 2
DEC_BATCH = 128
DEC_SEQ = 1
PAST_LEN = 2048
PAGE_SIZE = 128

D_CONV = D_MODEL // 2
CONV_W = 31
HD_FOX = 128
H_FOX = D_MODEL // 256
D_FOX = H_FOX * HD_FOX
H_MEM = 4
HD_MEM = D_MODEL // 8
D_MEM = H_MEM * HD_MEM
N_MEM = 256
N_BRANCH = 3
D_FF = 4 * D_MODEL
Q_BLOCK = 128
LN_EPS = 1e-5
ALPHA = (2 * DEPTH) ** 0.25
BETA = (8 * DEPTH) ** -0.25
IN_COLS = 2 * D_CONV + 3 * D_FOX + H_FOX + D_MEM + N_BRANCH * D_MODEL
SPLIT_IDX = [int(i) for i in np.cumsum([2 * D_CONV, D_FOX, D_FOX, D_FOX, H_FOX, D_MEM])]

kernel_name = 'hybrid_conformer_fox_memory_decoder_step'


def layer_norm(x, g, b):
    xf = x.astype(jnp.float32)
    mu = jnp.mean(xf, axis=-1, keepdims=True)
    var = jnp.mean(jnp.square(xf - mu), axis=-1, keepdims=True)
    return ((xf - mu) * lax.rsqrt(var + LN_EPS)).astype(x.dtype) * g + b


def in_projection(h, w_in, b_f, b_gate):
    B, T = h.shape[:2]
    z = jnp.einsum('btd,dc->btc', h, w_in)
    glu_in, q, k, v, f_pre, q_mem, gate_pre = jnp.split(z, SPLIT_IDX, axis=-1)
    u = glu_in[..., :D_CONV] * jax.nn.sigmoid(glu_in[..., D_CONV:])
    q = q.reshape(B, T, H_FOX, HD_FOX)
    k = k.reshape(B, T, H_FOX, HD_FOX)
    v = v.reshape(B, T, H_FOX, HD_FOX)
    logf = jax.nn.log_sigmoid((f_pre + b_f).astype(jnp.float32))
    q_mem = q_mem.reshape(B, T, H_MEM, HD_MEM)
    gates = jax.nn.sigmoid(gate_pre + b_gate).reshape(B, T, N_BRANCH, D_MODEL)
    return u, q, k, v, logf, q_mem, gates


def conv_branch(u_ext, w_dw, b_dw, g, b, w_out):
    h = lax.conv_general_dilated(u_ext, w_dw[:, None, :].astype(u_ext.dtype), window_strides=(1,),
                                 padding='VALID', dimension_numbers=('NWC', 'WIO', 'NWC'),
                                 feature_group_count=D_CONV) + b_dw
    h = jax.nn.silu(layer_norm(h, g, b))
    return jnp.einsum('btc,cd->btd', h, w_out)


def fox_attend(q, k, v, cq, ck, q_pos, k_pos):
    B, Tq, H, Dh = q.shape
    nb = Tq // Q_BLOCK if (Tq % Q_BLOCK == 0) else 1
    bq = Tq // nb
    scale = Dh ** -0.5
    ck_t = jnp.transpose(ck, (0, 2, 1))

    def block(args):
        qb, cqb, pb = args
        s = jnp.einsum('bqhd,bkhd->bhqk', qb, k).astype(jnp.float32) * scale
        s = s + (jnp.transpose(cqb, (0, 2, 1))[:, :, :, None] - ck_t[:, :, None, :])
        s = jnp.where(pb[:, None] >= k_pos[None, :], s, -jnp.inf)
        p = jax.nn.softmax(s, axis=-1).astype(v.dtype)
        return jnp.einsum('bhqk,bkhd->bqhd', p, v)

    qs = q.reshape(B, nb, bq, H, Dh).transpose(1, 0, 2, 3, 4)
    cqs = cq.reshape(B, nb, bq, H).transpose(1, 0, 2, 3)
    ps = q_pos.reshape(nb, bq)
    out = lax.map(block, (qs, cqs, ps))
    return out.transpose(1, 0, 2, 3, 4).reshape(B, Tq, H * Dh)


def mem_kv(mem, w_mem_kv):
    B, N = mem.shape[:2]
    kv = jnp.einsum('bnd,dc->bnc', mem, w_mem_kv)
    mk, mv = jnp.split(kv, 2, axis=-1)
    return mk.reshape(B, N, H_MEM, HD_MEM), mv.reshape(B, N, H_MEM, HD_MEM)


def mem_attend(q, mk, mv):
    B, T = q.shape[:2]
    s = jnp.einsum('bqhd,bkhd->bhqk', q, mk).astype(jnp.float32) * (HD_MEM ** -0.5)
    p = jax.nn.softmax(s, axis=-1).astype(mv.dtype)
    return jnp.einsum('bhqk,bkhd->bqhd', p, mv).reshape(B, T, D_MEM)


def merge_and_mlp(x, gates, conv_o, fox_o, mem_o, w_o, ln1_g, ln1_b, w_up, w_down, ln2_g, ln2_b):
    mix = gates[:, :, 0] * conv_o + gates[:, :, 1] * fox_o + gates[:, :, 2] * mem_o
    x = layer_norm(ALPHA * x + jnp.einsum('btd,de->bte', mix, w_o), ln1_g, ln1_b)
    h = jnp.square(jax.nn.relu(jnp.einsum('btd,df->btf', x, w_up)))
    return layer_norm(ALPHA * x + jnp.einsum('btf,fd->btd', h, w_down), ln2_g, ln2_b)


def setup_inputs(seed: int = 0) -> dict:
    key = jax.random.key(seed)
    ks = jax.random.split(key, 40)
    n_pages = PAST_LEN // PAGE_SIZE
    n_pool = (5 * DEC_BATCH * n_pages) // 4
    nrm = lambda k, shape, s=1.0: jax.random.normal(k, shape, jnp.float32) * s
    perm = jax.random.permutation(ks[0], n_pool)[: DEC_BATCH * n_pages]
    page_table = perm.reshape(DEC_BATCH, n_pages).astype(jnp.int32)
    return {
        'x_prompt': nrm(ks[1], (BATCH, SEQ, D_MODEL)),
        'x_sample': nrm(ks[2], (DEC_BATCH, DEC_SEQ, D_MODEL)),
        'mem_prompt': nrm(ks[3], (BATCH, N_MEM, D_MODEL)),
        'cache_k': nrm(ks[4], (DEPTH, n_pool, PAGE_SIZE, H_FOX, HD_FOX)),
        'cache_v': nrm(ks[5], (DEPTH, n_pool, PAGE_SIZE, H_FOX, HD_FOX)),
        'cache_logf': jax.nn.log_sigmoid(3.0 + nrm(ks[6], (DEPTH, n_pool, PAGE_SIZE, H_FOX))),
        'page_table': page_table,
        'state_conv': nrm(ks[7], (DEPTH, DEC_BATCH, CONV_W - 1, D_CONV), 0.5),
        'cache_mem_k': nrm(ks[8], (DEPTH, DEC_BATCH, N_MEM, H_MEM, HD_MEM)),
        'cache_mem_v': nrm(ks[9], (DEPTH, DEC_BATCH, N_MEM, H_MEM, HD_MEM)),
        'w_in': nrm(ks[10], (DEPTH, D_MODEL, IN_COLS), D_MODEL ** -0.5),
        'b_f': 3.0 + nrm(ks[11], (DEPTH, H_FOX), 0.5),
        'b_gate': nrm(ks[12], (DEPTH, N_BRANCH * D_MODEL), 0.1),
        'w_dw': nrm(ks[13], (DEPTH, CONV_W, D_CONV), CONV_W ** -0.5),
        'b_dw': nrm(ks[14], (DEPTH, D_CONV), 0.02),
        'conv_ln_g': 1.0 + nrm(ks[15], (DEPTH, D_CONV), 0.02),
        'conv_ln_b': nrm(ks[16], (DEPTH, D_CONV), 0.02),
        'w_conv_out': nrm(ks[17], (DEPTH, D_CONV, D_MODEL), BETA * D_CONV ** -0.5),
        'w_fox_out': nrm(ks[18], (DEPTH, D_FOX, D_MODEL), BETA * D_FOX ** -0.5),
        'w_mem_kv': nrm(ks[19], (DEPTH, D_MODEL, 2 * D_MEM), D_MODEL ** -0.5),
        'w_mem_out': nrm(ks[20], (DEPTH, D_MEM, D_MODEL), BETA * D_MEM ** -0.5),
        'w_o': nrm(ks[21], (DEPTH, D_MODEL, D_MODEL), BETA * D_MODEL ** -0.5),
        'ln1_g': 1.0 + nrm(ks[22], (DEPTH, D_MODEL), 0.02),
        'ln1_b': nrm(ks[23], (DEPTH, D_MODEL), 0.02),
        'w_up': nrm(ks[24], (DEPTH, D_MODEL, D_FF), D_MODEL ** -0.5),
        'w_down': nrm(ks[25], (DEPTH, D_FF, D_MODEL), BETA * D_FF ** -0.5),
        'ln2_g': 1.0 + nrm(ks[26], (DEPTH, D_MODEL), 0.02),
        'ln2_b': nrm(ks[27], (DEPTH, D_MODEL), 0.02),
    }


def reference(x_prompt, x_sample, mem_prompt, cache_k, cache_v, cache_logf, page_table, state_conv,
              cache_mem_k, cache_mem_v, w_in, b_f, b_gate, w_dw, b_dw, conv_ln_g, conv_ln_b,
              w_conv_out, w_fox_out, w_mem_kv, w_mem_out, w_o, ln1_g, ln1_b, w_up, w_down, ln2_g, ln2_b):
    Tp = x_prompt.shape[1]
    Bs, Ts = x_sample.shape[:2]
    n_pages = page_table.shape[1]
    past = n_pages * cache_k.shape[2]
    pos_p = jnp.arange(Tp)
    pos_sq = past + jnp.arange(Ts)
    pos_sk = jnp.arange(past + Ts)

    xp, xs = x_prompt, x_sample
    kp_l, vp_l, fp_l, cp_l, mkp_l, mvp_l = [], [], [], [], [], []
    ks_l, vs_l, fs_l, cs_l = [], [], [], []
    for l in range(DEPTH):
        u, q, k, v, logf, qm, gates = in_projection(xp, w_in[l], b_f[l], b_gate[l])
        u_ext = jnp.pad(u, ((0, 0), (CONV_W - 1, 0), (0, 0)))
        conv_o = conv_branch(u_ext, w_dw[l], b_dw[l], conv_ln_g[l], conv_ln_b[l], w_conv_out[l])
        c = jnp.cumsum(logf, axis=1)
        fox_o = jnp.einsum('btc,cd->btd', fox_attend(q, k, v, c, c, pos_p, pos_p), w_fox_out[l])
        mk, mv = mem_kv(mem_prompt, w_mem_kv[l])
        mem_o = jnp.einsum('btc,cd->btd', mem_attend(qm, mk, mv), w_mem_out[l])
        xp = merge_and_mlp(xp, gates, conv_o, fox_o, mem_o, w_o[l], ln1_g[l], ln1_b[l],
                           w_up[l], w_down[l], ln2_g[l], ln2_b[l])
        kp_l.append(k); vp_l.append(v); fp_l.append(logf)
        cp_l.append(u_ext[:, -(CONV_W - 1):]); mkp_l.append(mk); mvp_l.append(mv)

        u, q, k, v, logf, qm, gates = in_projection(xs, w_in[l], b_f[l], b_gate[l])
        u_ext = jnp.concatenate([state_conv[l].astype(u.dtype), u], axis=1)
        conv_o = conv_branch(u_ext, w_dw[l], b_dw[l], conv_ln_g[l], conv_ln_b[l], w_conv_out[l])
        k_past = jnp.take(cache_k[l], page_table, axis=0).reshape(Bs, past, H_FOX, HD_FOX)
        v_past = jnp.take(cache_v[l], page_table, axis=0).reshape(Bs, past, H_FOX, HD_FOX)
        f_past = jnp.take(cache_logf[l], page_table, axis=0).reshape(Bs, past, H_FOX)
        k_all = jnp.concatenate([k_past.astype(k.dtype), k], axis=1)
        v_all = jnp.concatenate([v_past.astype(v.dtype), v], axis=1)
        c_all = jnp.cumsum(jnp.concatenate([f_past.astype(jnp.float32), logf], axis=1), axis=1)
        fox_o = jnp.einsum('btc,cd->btd',
                           fox_attend(q, k_all, v_all, c_all[:, past:], c_all, pos_sq, pos_sk), w_fox_out[l])
        mem_o = jnp.einsum('btc,cd->btd',
                           mem_attend(qm, cache_mem_k[l].astype(qm.dtype), cache_mem_v[l].astype(qm.dtype)),
                           w_mem_out[l])
        xs = merge_and_mlp(xs, gates, conv_o, fox_o, mem_o, w_o[l], ln1_g[l], ln1_b[l],
                           w_up[l], w_down[l], ln2_g[l], ln2_b[l])
        ks_l.append(k); vs_l.append(v); fs_l.append(logf); cs_l.append(u_ext[:, -(CONV_W - 1):])

    new_k_prompt = jnp.stack(kp_l)
    new_v_prompt = jnp.stack(vp_l)
    new_logf_prompt = jnp.stack(fp_l)
    new_conv_prompt = jnp.stack(cp_l)
    new_mem_k_prompt = jnp.stack(mkp_l)
    new_mem_v_prompt = jnp.stack(mvp_l)
    new_k_sample = jnp.stack(ks_l)
    new_v_sample = jnp.stack(vs_l)
    new_logf_sample = jnp.stack(fs_l)
    new_conv_sample = jnp.stack(cs_l)
    return (xp, xs, new_k_prompt, new_v_prompt, new_logf_prompt, new_conv_prompt,
            new_mem_k_prompt, new_mem_v_prompt, new_k_sample, new_v_sample, new_logf_sample, new_conv_sample)
```

```python
import functools

import jax
import jax.numpy as jnp
from jax import lax
from jax.experimental import pallas as pl
from jax.experimental.pallas import tpu as pltpu

F32 = jnp.float32
BF16 = jnp.bfloat16

LN_EPS = 1e-5
LANES = 128
SUBLANES = 8
VMEM_LIMIT_BYTES = 56 * 1024 * 1024
CONV_HALO = 32


def _cparams(*sem):
    return pltpu.CompilerParams(dimension_semantics=sem, vmem_limit_bytes=VMEM_LIMIT_BYTES)


def _tile(n, pref, mult):
    t = min(pref, n)
    t -= t % mult
    while t >= mult:
        if n % t == 0:
            return t
        t -= mult
    return n


def _layer_norm(r, g, b):
    mu = jnp.mean(r, axis=-1, keepdims=True)
    d = r - mu
    var = jnp.mean(d * d, axis=-1, keepdims=True)
    return d * lax.rsqrt(var + LN_EPS) * g + b


def _split3(f):
    hi = f.astype(BF16)
    r1 = f - hi.astype(F32)
    mid = r1.astype(BF16)
    lo = (r1 - mid.astype(F32)).astype(BF16)
    return hi, mid, lo


def _dot(a, b):
    return jnp.dot(a, b, preferred_element_type=F32)


def _dot_nt(a, b):
    return lax.dot_general(a, b, (((1,), (1,)), ((), ())), preferred_element_type=F32)


def _mm_body(*refs, n_w, epilogue, has_bias):
    x_ref = refs[0]
    w_refs = refs[1:1 + n_w]
    pos = 1 + n_w
    b_ref = refs[pos] if has_bias else None
    o_refs = refs[pos + int(has_bias):]
    x = x_ref[...]
    zs = [_dot(x, w[...]) for w in w_refs]
    if epilogue == "glu":
        y = zs[0] * jax.nn.sigmoid(zs[1])
    elif epilogue == "logsig":
        y = jax.nn.log_sigmoid(zs[0] + b_ref[...])
    else:
        y = zs[0]
    for o in o_refs:
        o[...] = y.astype(o.dtype)


def _mm(x, w, col_starts, n, out_dtypes, *, epilogue="none", bias=None, tm_pref=1024, tn_pref=1024, name="mm"):
    M, K = x.shape
    tm = _tile(M, tm_pref, 16)
    tn = _tile(n, tn_pref, LANES)
    for c in col_starts:
        assert c % tn == 0
    in_specs = [pl.BlockSpec((tm, K), lambda i, j: (i, 0))]
    args = [x]
    for c in col_starts:
        in_specs.append(pl.BlockSpec((K, tn), lambda i, j, off=c // tn: (0, j + off)))
        args.append(w)
    if bias is not None:
        in_specs.append(pl.BlockSpec((1, tn), lambda i, j: (0, j)))
        args.append(bias)
    out_shape = [jax.ShapeDtypeStruct((M, n), dt) for dt in out_dtypes]
    out_specs = [pl.BlockSpec((tm, tn), lambda i, j: (i, j)) for _ in out_dtypes]
    outs = pl.pallas_call(
        functools.partial(_mm_body, n_w=len(col_starts), epilogue=epilogue, has_bias=bias is not None),
        grid=(M // tm, n // tn), in_specs=in_specs, out_specs=out_specs, out_shape=out_shape,
        compiler_params=_cparams("parallel", "parallel"), name=name)(*args)
    return outs


def _cumsum_body(f_ref, ccol_ref, crow_ref, *, chunk):
    T = f_ref.shape[0]
    r = lax.broadcasted_iota(jnp.int32, (chunk, chunk), 0)
    c = lax.broadcasted_iota(jnp.int32, (chunk, chunk), 1)
    tri = jnp.where(c <= r, 1.0, 0.0).astype(BF16)
    carry = jnp.zeros((1, f_ref.shape[1]), F32)
    for ci in range(T // chunk):
        hi, mid, lo = _split3(f_ref[ci * chunk:(ci + 1) * chunk, :])
        cs = _dot(tri, hi) + _dot(tri, mid) + _dot(tri, lo) + carry
        ccol_ref[ci * chunk:(ci + 1) * chunk, :] = cs
        carry = cs[chunk - 1:chunk, :]
    crow_ref[...] = ccol_ref[...].T


def _cumsum(logf_pad, B, T):
    chunk = _tile(T, 256, LANES)
    return pl.pallas_call(
        functools.partial(_cumsum_body, chunk=chunk),
        grid=(B,),
        in_specs=[pl.BlockSpec((T, LANES), lambda b: (b, 0))],
        out_specs=[pl.BlockSpec((T, LANES), lambda b: (b, 0)),
                   pl.BlockSpec((None, LANES, T), lambda b: (b, 0, 0))],
        out_shape=[jax.ShapeDtypeStruct((B * T, LANES), F32), jax.ShapeDtypeStruct((B, LANES, T), F32)],
        compiler_params=_cparams("parallel"), name="logf_cumsum")(logf_pad)


def _conv_body(prev_ref, cur_ref, w_ref, bdw_ref, g_ref, b_ref, o_ref, sh_ref, *, rc, copy_rows):
    i = pl.program_id(1)
    tt, C = cur_ref.shape
    W = w_ref.shape[0]
    sh_ref[0, 0:CONV_HALO, :] = jnp.where(i > 0, prev_ref[...], 0.0)
    sh_ref[0, CONV_HALO:, :] = cur_ref[...]
    n_shifted = tt + CONV_HALO - SUBLANES
    for s in range(1, SUBLANES):
        for r in range(0, n_shifted, copy_rows):
            sh_ref[s, r:r + copy_rows, :] = sh_ref[0, r + s:r + s + copy_rows, :]
    first = CONV_HALO - (W - 1)

    def rows(r, carry):
        r0 = pl.multiple_of(r * rc, rc)

        def tap(j):
            e = first + j
            return w_ref[j:j + 1, :] * sh_ref[e % SUBLANES, pl.ds(r0 + (e - e % SUBLANES), rc), :]

        acc = tap(0)
        for j in range(1, W):
            acc = acc + tap(j)
        y = _layer_norm(acc + bdw_ref[...], g_ref[...], b_ref[...])
        o_ref[pl.ds(r0, rc), :] = (y * jax.nn.sigmoid(y)).astype(o_ref.dtype)
        return carry

    lax.fori_loop(0, tt // rc, rows, 0)


def _conv_prompt(u, w_dw, b_dw, g, b, B, T):
    C = u.shape[1]
    W = w_dw.shape[0]
    assert W - 1 <= CONV_HALO and T % CONV_HALO == 0
    tt = _tile(T, 256, CONV_HALO)
    nt = T // tt
    copy_rows = _tile(tt + CONV_HALO - SUBLANES, 64, SUBLANES)
    return pl.pallas_call(
        functools.partial(_conv_body, rc=16, copy_rows=copy_rows),
        grid=(B, nt),
        in_specs=[pl.BlockSpec((CONV_HALO, C), lambda bi, i: (jnp.maximum((bi * T + i * tt) // CONV_HALO - 1, 0), 0)),
                  pl.BlockSpec((tt, C), lambda bi, i: (bi * nt + i, 0)),
                  pl.BlockSpec((W, C), lambda bi, i: (0, 0)),
                  pl.BlockSpec((1, C), lambda bi, i: (0, 0)),
                  pl.BlockSpec((1, C), lambda bi, i: (0, 0)),
                  pl.BlockSpec((1, C), lambda bi, i: (0, 0))],
        out_specs=pl.BlockSpec((tt, C), lambda bi, i: (bi * nt + i, 0)),
        out_shape=jax.ShapeDtypeStruct((B * T, C), BF16),
        scratch_shapes=[pltpu.VMEM((SUBLANES, CONV_HALO + tt, C), F32)],
        compiler_params=_cparams("parallel", "parallel"), name="conv_prompt")(u, u, w_dw, b_dw, g, b)


def _fox_body(q_ref, k_ref, v_ref, ccol_ref, crow_ref, o_ref, *, H, Dh, scale):
    qi = pl.program_id(1)
    tq = q_ref.shape[0]
    row = lax.broadcasted_iota(jnp.int32, (tq, tq), 0)
    col = lax.broadcasted_iota(jnp.int32, (tq, tq), 1)
    causal = row >= col
    for h in range(H):
        sl = slice(h * Dh, (h + 1) * Dh)
        q = q_ref[:, sl]
        cq = ccol_ref[:, h:h + 1]

        def step(j, carry, masked):
            m, l, acc = carry
            k0 = pl.multiple_of(j * tq, tq)
            s = _dot_nt(q, k_ref[pl.ds(k0, tq), sl]) * scale
            s = s + (cq - crow_ref[h:h + 1, pl.ds(k0, tq)])
            if masked:
                s = jnp.where(causal, s, -jnp.inf)
            m_new = jnp.maximum(m, jnp.max(s, axis=-1, keepdims=True))
            a = jnp.exp(m - m_new)
            p = jnp.exp(s - m_new)
            l = a * l + jnp.sum(p, axis=-1, keepdims=True)
            acc = a * acc + _dot(p.astype(BF16), v_ref[pl.ds(k0, tq), sl])
            return m_new, l, acc

        init = (jnp.full((tq, 1), -jnp.inf, F32), jnp.zeros((tq, 1), F32), jnp.zeros((tq, Dh), F32))
        carry = lax.fori_loop(0, qi, functools.partial(step, masked=False), init)
        _, l, acc = step(qi, carry, True)
        o_ref[:, sl] = (acc / l).astype(o_ref.dtype)


def _fox_prompt(q, k, v, ccol, crow, B, T, H, Dh):
    DF = H * Dh
    tq = _tile(T, 256, LANES)
    nq = T // tq
    return pl.pallas_call(
        functools.partial(_fox_body, H=H, Dh=Dh, scale=Dh ** -0.5),
        grid=(B, nq),
        in_specs=[pl.BlockSpec((tq, DF), lambda b, i: (b * nq + i, 0)),
                  pl.BlockSpec((T, DF), lambda b, i: (b, 0)),
                  pl.BlockSpec((T, DF), lambda b, i: (b, 0)),
                  pl.BlockSpec((tq, LANES), lambda b, i: (b * nq + i, 0)),
                  pl.BlockSpec((None, SUBLANES, T), lambda b, i: (b, 0, 0))],
        out_specs=pl.BlockSpec((tq, DF), lambda b, i: (b * nq + i, 0)),
        out_shape=jax.ShapeDtypeStruct((B * T, DF), BF16),
        compiler_params=_cparams("parallel", "parallel"), name="fox_prompt")(q, k, v, ccol, crow)


def _memattn_body(q_ref, k_ref, v_ref, o_ref, *, H, Dh, scale):
    for h in range(H):
        sl = slice(h * Dh, (h + 1) * Dh)
        s = _dot_nt(q_ref[:, sl], k_ref[:, sl]) * scale
        e = jnp.exp(s - jnp.max(s, axis=-1, keepdims=True))
        p = e / jnp.sum(e, axis=-1, keepdims=True)
        o_ref[:, sl] = _dot(p.astype(BF16), v_ref[:, sl]).astype(o_ref.dtype)


def _memattn_prompt(q, mk, mv, B, T, H, Dh):
    DM = H * Dh
    N = mk.shape[0] // B
    tq = _tile(T, 512, LANES)
    nq = T // tq
    return pl.pallas_call(
        functools.partial(_memattn_body, H=H, Dh=Dh, scale=Dh ** -0.5),
        grid=(B, nq),
        in_specs=[pl.BlockSpec((tq, DM), lambda b, i: (b * nq + i, 0)),
                  pl.BlockSpec((N, DM), lambda b, i: (b, 0)),
                  pl.BlockSpec((N, DM), lambda b, i: (b, 0))],
        out_specs=pl.BlockSpec((tq, DM), lambda b, i: (b * nq + i, 0)),
        out_shape=jax.ShapeDtypeStruct((B * T, DM), BF16),
        compiler_params=_cparams("parallel", "parallel"), name="memattn_prompt")(q, mk, mv)


def _mix_body(x_ref, *refs, nb):
    h_refs, w_refs, wg_refs, bg_refs = (refs[i * nb:(i + 1) * nb] for i in range(4))
    o_ref = refs[4 * nb]
    x = x_ref[...]
    mix = None
    for h_ref, w_ref, wg_ref, bg_ref in zip(h_refs, w_refs, wg_refs, bg_refs):
        term = jax.nn.sigmoid(_dot(x, wg_ref[...]) + bg_ref[...]) * _dot(h_ref[...], w_ref[...])
        mix = term if mix is None else mix + term
    o_ref[...] = mix.astype(o_ref.dtype)


def _mix(x16, branches, w_outs, w_tail, gate_col0, b_gate, *, tm_pref=512, tn_pref=512):
    M, D = x16.shape
    nb = len(branches)
    tm = _tile(M, tm_pref, 16)
    tn = _tile(D, tn_pref, LANES)
    assert gate_col0 % tn == 0
    in_specs = [pl.BlockSpec((tm, D), lambda i, j: (i, 0))]
    in_specs += [pl.BlockSpec((tm, h.shape[1]), lambda i, j: (i, 0)) for h in branches]
    in_specs += [pl.BlockSpec((w.shape[0], tn), lambda i, j: (0, j)) for w in w_outs]
    in_specs += [pl.BlockSpec((D, tn), lambda i, j, off=(gate_col0 + b * D) // tn: (0, j + off)) for b in range(nb)]
    in_specs += [pl.BlockSpec((1, tn), lambda i, j, off=(b * D) // tn: (0, j + off)) for b in range(nb)]
    return pl.pallas_call(
        functools.partial(_mix_body, nb=nb),
        grid=(M // tm, D // tn), in_specs=in_specs,
        out_specs=pl.BlockSpec((tm, tn), lambda i, j: (i, j)),
        out_shape=jax.ShapeDtypeStruct((M, D), BF16),
        compiler_params=_cparams("parallel", "parallel"), name="gated_mix",
    )(x16, *branches, *w_outs, *([w_tail] * nb), *([b_gate] * nb))


def _wo_ln_body(mix_ref, w_ref, x_ref, g_ref, b_ref, o32_ref, o16_ref, *, alpha):
    y = _layer_norm(alpha * x_ref[...] + _dot(mix_ref[...], w_ref[...]), g_ref[...], b_ref[...])
    o32_ref[...] = y
    o16_ref[...] = y.astype(o16_ref.dtype)


def _wo_ln(mix, w_o, x32, g, b, alpha, *, tm_pref=512):
    M, D = x32.shape
    tm = _tile(M, tm_pref, 16)
    row = lambda i: (i, 0)
    fixed = lambda i: (0, 0)
    return pl.pallas_call(
        functools.partial(_wo_ln_body, alpha=alpha),
        grid=(M // tm,),
        in_specs=[pl.BlockSpec((tm, D), row), pl.BlockSpec((D, D), fixed), pl.BlockSpec((tm, D), row),
                  pl.BlockSpec((1, D), fixed), pl.BlockSpec((1, D), fixed)],
        out_specs=[pl.BlockSpec((tm, D), row), pl.BlockSpec((tm, D), row)],
        out_shape=[jax.ShapeDtypeStruct((M, D), F32), jax.ShapeDtypeStruct((M, D), BF16)],
        compiler_params=_cparams("parallel"), name="wo_ln")(mix, w_o, x32, g, b)


def _mlp_body(x_ref, wu_ref, wd_ref, g_ref, b_ref, o32_ref, o16_ref, xb_ref, *, alpha):
    f = pl.program_id(1)

    @pl.when(f == 0)
    def _():
        xb_ref[...] = x_ref[...].astype(BF16)

    h = jnp.maximum(_dot(xb_ref[...], wu_ref[...]), 0.0)
    y = _dot((h * h).astype(BF16), wd_ref[...])

    @pl.when(f == 0)
    def _():
        o32_ref[...] = y

    @pl.when(f > 0)
    def _():
        o32_ref[...] += y

    @pl.when(f == pl.num_programs(1) - 1)
    def _():
        out = _layer_norm(alpha * x_ref[...] + o32_ref[...], g_ref[...], b_ref[...])
        o32_ref[...] = out
        o16_ref[...] = out.astype(o16_ref.dtype)


def _mlp(x32, w_up, w_down, g, b, alpha, *, tm_pref=512, tf_pref=512):
    M, D = x32.shape
    DFF = w_up.shape[1]
    tm = _tile(M, tm_pref, 16)
    tf = _tile(DFF, tf_pref, LANES)
    row = lambda i, f: (i, 0)
    fixed = lambda i, f: (0, 0)
    return pl.pallas_call(
        functools.partial(_mlp_body, alpha=alpha),
        grid=(M // tm, DFF // tf),
        in_specs=[pl.BlockSpec((tm, D), row), pl.BlockSpec((D, tf), lambda i, f: (0, f)),
                  pl.BlockSpec((tf, D), lambda i, f: (f, 0)),
                  pl.BlockSpec((1, D), fixed), pl.BlockSpec((1, D), fixed)],
        out_specs=[pl.BlockSpec((tm, D), row), pl.BlockSpec((tm, D), row)],
        out_shape=[jax.ShapeDtypeStruct((M, D), F32), jax.ShapeDtypeStruct((M, D), BF16)],
        scratch_shapes=[pltpu.VMEM((tm, D), BF16)],
        compiler_params=_cparams("parallel", "arbitrary"), name="mlp_ln")(x32, w_up, w_down, g, b)


def _conv_sample_body(s_ref, u_ref, w_ref, bdw_ref, g_ref, b_ref, h_ref, ns_ref):
    s = s_ref[...]
    u = u_ref[...]
    Wm1 = s.shape[1]
    w = w_ref[...]
    acc = jnp.sum(s * w[None, :Wm1, :], axis=1, keepdims=True) + u * w[None, Wm1:, :]
    y = _layer_norm(acc + bdw_ref[...][None], g_ref[...][None], b_ref[...][None])
    h_ref[...] = y * jax.nn.sigmoid(y)
    ns_ref[:, 0:Wm1 - 1, :] = s[:, 1:, :]
    ns_ref[:, Wm1 - 1:Wm1, :] = u


def _conv_sample(state, l, u, w_dw, b_dw, g, b):
    _, Bs, Wm1, C = state.shape
    bt = _tile(Bs, 16, 1)
    blk = lambda i: (i, 0, 0)
    fixed = lambda i: (0, 0)
    return pl.pallas_call(
        _conv_sample_body,
        grid=(Bs // bt,),
        in_specs=[pl.BlockSpec((None, bt, Wm1, C), lambda i: (l, i, 0, 0)), pl.BlockSpec((bt, 1, C), blk),
                  pl.BlockSpec((Wm1 + 1, C), fixed), pl.BlockSpec((1, C), fixed),
                  pl.BlockSpec((1, C), fixed), pl.BlockSpec((1, C), fixed)],
        out_specs=[pl.BlockSpec((bt, 1, C), blk), pl.BlockSpec((bt, Wm1, C), blk)],
        out_shape=[jax.ShapeDtypeStruct((Bs, 1, C), F32), jax.ShapeDtypeStruct((Bs, Wm1, C), F32)],
        compiler_params=_cparams("parallel"), name="conv_sample")(state, u.reshape(Bs, 1, C), w_dw, b_dw, g, b)


def _scores3(k3, qs):
    return jnp.sum(k3 * qs[None], axis=-1, keepdims=True)


def _online_update(state, s3, v3):
    m, l, acc = state
    m_new = jnp.maximum(m, jnp.max(s3, axis=0))
    a = jnp.exp(m - m_new)
    e3 = jnp.exp(s3 - m_new[None])
    return m_new, a * l + jnp.sum(e3, axis=0), a * acc + jnp.sum(e3 * v3, axis=0)


def _softmax_state(H, Dh):
    return jnp.full((H, 1), -jnp.inf, F32), jnp.zeros((H, 1), F32), jnp.zeros((H, Dh), F32)


def _fox_sample_body(pt_ref, q_ref, kn_ref, vn_ref, fn_ref, *refs, n_pages, scale):
    del pt_ref
    k_refs, v_refs, f_refs = (refs[i * n_pages:(i + 1) * n_pages] for i in range(3))
    o_ref = refs[3 * n_pages]
    H, Dh = q_ref.shape
    PS = k_refs[0].shape[0]
    qs = q_ref[...] * scale
    later = lax.broadcasted_iota(jnp.int32, (PS, H, PS), 2) > lax.broadcasted_iota(jnp.int32, (PS, H, PS), 0)
    ones = jnp.ones((2 * PS, LANES), BF16)
    after = fn_ref[...]
    state = _softmax_state(H, Dh)
    for p in range(n_pages - 1, -1, -1):
        ft = f_refs[p][...]
        hi = ft.astype(BF16).astype(F32)
        parts = [jnp.where(later, x[None], 0.0).reshape(PS * H, PS).astype(BF16) for x in (hi, ft - hi)]
        suffix = _dot(jnp.concatenate(parts, axis=1), ones).reshape(PS, H, LANES)[:, :, 0:1]
        s3 = _scores3(k_refs[p][...], qs) + (suffix + after[None])
        state = _online_update(state, s3, v_refs[p][...])
        after = after + jnp.sum(ft, axis=-1, keepdims=True)
    s_new = jnp.sum(kn_ref[...] * qs, axis=-1, keepdims=True)
    _, l, acc = _online_update(state, s_new[None], vn_ref[...][None])
    o_ref[...] = acc / l


def _fox_sample(q, k_new, v_new, logf_new, cache_k, cache_v, cache_logf_t, l, page_table):
    Bs, H, Dh = q.shape
    PS = cache_k.shape[2]
    n_pages = page_table.shape[1]
    one = lambda width: pl.BlockSpec((None, H, width), lambda b, pt: (b, 0, 0))
    kv_page = lambda p: pl.BlockSpec((None, None, PS, H, Dh), lambda b, pt: (l, pt[b, p], 0, 0, 0))
    f_page = lambda p: pl.BlockSpec((None, None, H, PS), lambda b, pt: (l, pt[b, p], 0, 0))
    in_specs = [one(Dh), one(Dh), one(Dh), one(1)]
    in_specs += [kv_page(p) for p in range(n_pages)] + [kv_page(p) for p in range(n_pages)]
    in_specs += [f_page(p) for p in range(n_pages)]
    return pl.pallas_call(
        functools.partial(_fox_sample_body, n_pages=n_pages, scale=Dh ** -0.5),
        grid_spec=pltpu.PrefetchScalarGridSpec(
            num_scalar_prefetch=1, grid=(Bs,), in_specs=in_specs,
            out_specs=pl.BlockSpec((None, H, Dh), lambda b, pt: (b, 0, 0))),
        out_shape=jax.ShapeDtypeStruct((Bs, H, Dh), F32),
        compiler_params=_cparams("arbitrary"), name="fox_sample",
    )(page_table, q, k_new, v_new, logf_new,
      *([cache_k] * n_pages), *([cache_v] * n_pages), *([cache_logf_t] * n_pages))


def _mem_sample_body(q_ref, k_ref, v_ref, o_ref, *, scale):
    G, H, Dh = q_ref.shape
    for g in range(G):
        qs = q_ref[g] * scale
        _, l, acc = _online_update(_softmax_state(H, Dh), _scores3(k_ref[g], qs), v_ref[g])
        o_ref[g] = acc / l


def _mem_sample(q, mem_k, mem_v, l):
    _, Bs, N, H, Dh = mem_k.shape
    G = _tile(Bs, 4, 1)
    blk = lambda i: (i, 0, 0)
    cache = pl.BlockSpec((None, G, N, H, Dh), lambda i: (l, i, 0, 0, 0))
    return pl.pallas_call(
        functools.partial(_mem_sample_body, scale=Dh ** -0.5),
        grid=(Bs // G,),
        in_specs=[pl.BlockSpec((G, H, Dh), blk), cache, cache],
        out_specs=pl.BlockSpec((G, H, Dh), blk),
        out_shape=jax.ShapeDtypeStruct((Bs, H, Dh), F32),
        compiler_params=_cparams("parallel"), name="mem_sample")(q, mem_k, mem_v)


def kernel(x_prompt, x_sample, mem_prompt, cache_k, cache_v, cache_logf, page_table, state_conv, cache_mem_k, cache_mem_v, w_in, b_f, b_gate, w_dw, b_dw, conv_ln_g, conv_ln_b, w_conv_out, w_fox_out, w_mem_kv, w_mem_out, w_o, ln1_g, ln1_b, w_up, w_down, ln2_g, ln2_b):
    depth = w_in.shape[0]
    B, T, D = x_prompt.shape
    Bs, Ts, _ = x_sample.shape
    assert Ts == 1
    W, C = w_dw.shape[1:]
    H, Dh = cache_k.shape[3:]
    DF = H * Dh
    NM, Hm, Dhm = cache_mem_k.shape[2:]
    DM = Hm * Dhm
    n_pool, PS = cache_k.shape[1:3]
    NB = b_gate.shape[1] // D
    alpha = float((2 * depth) ** 0.25)

    c_q, c_k, c_v, c_f = 2 * C, 2 * C + DF, 2 * C + 2 * DF, 2 * C + 3 * DF
    c_tail = c_f + H

    w_main = w_in[:, :, :c_f].astype(BF16)
    w_f = jnp.pad(w_in[:, :, c_f:c_tail], ((0, 0), (0, 0), (0, LANES - H))).astype(BF16)
    w_tail = w_in[:, :, c_tail:].astype(BF16)
    b_f_pad = jnp.pad(b_f, ((0, 0), (0, LANES - H)))[:, None, :]
    w_co, w_fo, w_mo = w_conv_out.astype(BF16), w_fox_out.astype(BF16), w_mem_out.astype(BF16)
    w_mkv, w_o16 = w_mem_kv.astype(BF16), w_o.astype(BF16)
    w_up16, w_down16 = w_up.astype(BF16), w_down.astype(BF16)
    row = lambda a, l: a[l][None, :]

    def project(x16, l, name):
        u, = _mm(x16, w_main[l], [0, C], C, [F32], epilogue="glu", name=name + "_glu")
        q16, = _mm(x16, w_main[l], [c_q], DF, [BF16], name=name + "_q")
        k32, k16 = _mm(x16, w_main[l], [c_k], DF, [F32, BF16], name=name + "_k")
        v32, v16 = _mm(x16, w_main[l], [c_v], DF, [F32, BF16], name=name + "_v")
        logf_pad, = _mm(x16, w_f[l], [0], LANES, [F32], epilogue="logsig", bias=b_f_pad[l], name=name + "_logf")
        qm16, = _mm(x16, w_tail[l], [0], DM, [BF16], name=name + "_qm")
        return u, q16, k32, k16, v32, v16, logf_pad, qm16

    def merge(x32, x16, branches, l):
        mix = _mix(x16, branches, [w_co[l], w_fo[l], w_mo[l]], w_tail[l], DM, row(b_gate, l))
        x1_32, _ = _wo_ln(mix, w_o16[l], x32, row(ln1_g, l), row(ln1_b, l), alpha)
        return _mlp(x1_32, w_up16[l], w_down16[l], row(ln2_g, l), row(ln2_b, l), alpha)

    xp32 = x_prompt.reshape(B * T, D)
    xp16 = xp32.astype(BF16)
    xs32 = x_sample.reshape(Bs, D)
    xs16 = xs32.astype(BF16)
    mem16 = mem_prompt.reshape(B * NM, D).astype(BF16)
    cache_logf_t = jnp.swapaxes(cache_logf, 2, 3)
    outs = [[] for _ in range(10)]
    for l in range(depth):
        conv_args = (w_dw[l], row(b_dw, l), row(conv_ln_g, l), row(conv_ln_b, l))

        u, q16, k32, k16, v32, v16, logf_pad, qm16 = project(xp16, l, "prompt")
        hc = _conv_prompt(u, *conv_args, B, T)
        ccol, crow = _cumsum(logf_pad, B, T)
        fa = _fox_prompt(q16, k16, v16, ccol, crow, B, T, H, Dh)
        mk32, mk16 = _mm(mem16, w_mkv[l], [0], DM, [F32, BF16], name="mem_k")
        mv32, mv16 = _mm(mem16, w_mkv[l], [DM], DM, [F32, BF16], name="mem_v")
        ma = _memattn_prompt(qm16, mk16, mv16, B, T, Hm, Dhm)
        xp32, xp16 = merge(xp32, xp16, [hc, fa, ma], l)
        outs[0].append(k32.reshape(B, T, H, Dh))
        outs[1].append(v32.reshape(B, T, H, Dh))
        outs[2].append(logf_pad[:, :H].reshape(B, T, H))
        outs[3].append(u.reshape(B, T, C)[:, T - (W - 1):])
        outs[4].append(mk32.reshape(B, NM, Hm, Dhm))
        outs[5].append(mv32.reshape(B, NM, Hm, Dhm))

        u, q16, k32, k16, v32, v16, logf_pad, qm16 = project(xs16, l, "sample")
        hc3, new_state = _conv_sample(state_conv, l, u, *conv_args)
        heads = lambda a: a.astype(F32).reshape(Bs, H, Dh)
        fa = _fox_sample(heads(q16), heads(k32), heads(v32), logf_pad[:, :H, None], cache_k, cache_v, cache_logf_t,
                         l, page_table)
        ma = _mem_sample(qm16.astype(F32).reshape(Bs, Hm, Dhm), cache_mem_k, cache_mem_v, l)
        branches = [hc3.reshape(Bs, C), fa.reshape(Bs, DF), ma.reshape(Bs, DM)]
        xs32, xs16 = merge(xs32, xs16, [a.astype(BF16) for a in branches], l)
        outs[6].append(k32.reshape(Bs, Ts, H, Dh))
        outs[7].append(v32.reshape(Bs, Ts, H, Dh))
        outs[8].append(logf_pad[:, :H].reshape(Bs, Ts, H))
        outs[9].append(new_state)

    return (xp32.reshape(B, T, D), xs32.reshape(Bs, Ts, D)) + tuple(jnp.stack(o) for o in outs)
```

```python
import functools

import jax
import jax.numpy as jnp
from jax import lax
from jax.experimental import pallas as pl
from jax.experimental.pallas import tpu as pltpu

F32 = jnp.float32
BF16 = jnp.bfloat16

LN_EPS = 1e-5
LANES = 128
SUBLANES = 8
VMEM_LIMIT_BYTES = 56 * 1024 * 1024
CONV_HALO = 32


def _cparams(*sem):
    return pltpu.CompilerParams(dimension_semantics=sem, vmem_limit_bytes=VMEM_LIMIT_BYTES)


def _tile(n, pref, mult):
    t = min(pref, n)
    t -= t % mult
    while t >= mult:
        if n % t == 0:
            return t
        t -= mult
    return n


def _layer_norm(r, g, b):
    mu = jnp.mean(r, axis=-1, keepdims=True)
    d = r - mu
    var = jnp.mean(d * d, axis=-1, keepdims=True)
    return d * lax.rsqrt(var + LN_EPS) * g + b


def _split3(f):
    hi = f.astype(BF16)
    r1 = f - hi.astype(F32)
    mid = r1.astype(BF16)
    lo = (r1 - mid.astype(F32)).astype(BF16)
    return hi, mid, lo


def _dot(a, b):
    return jnp.dot(a, b, preferred_element_type=F32)


def _dot_nt(a, b):
    return lax.dot_general(a, b, (((1,), (1,)), ((), ())), preferred_element_type=F32)


def _mm_body(*refs, n_w, epilogue, has_bias):
    x_ref = refs[0]
    w_refs = refs[1:1 + n_w]
    pos = 1 + n_w
    b_ref = refs[pos] if has_bias else None
    o_refs = refs[pos + int(has_bias):]
    x = x_ref[...]
    zs = [_dot(x, w[...]) for w in w_refs]
    if epilogue == "glu":
        y = zs[0] * jax.nn.sigmoid(zs[1])
    elif epilogue == "logsig":
        y = jax.nn.log_sigmoid(zs[0] + b_ref[...])
    else:
        y = zs[0]
    for o in o_refs:
        if len(o.shape) == 3:
            Dh = o.shape[2]
            for h in range(o.shape[1]):
                o[:, h, :] = y[:, h * Dh:(h + 1) * Dh].astype(o.dtype)
        else:
            o[...] = y.astype(o.dtype)


def _mm(x, w, col_starts, n, out_dtypes, *, epilogue="none", bias=None, head_dim=None, tm_pref=1024, tn_pref=1024,
        name="mm"):
    M, K = x.shape
    tm = _tile(M, tm_pref, 16)
    tn = n if head_dim else _tile(n, tn_pref, LANES)
    for c in col_starts:
        assert c % tn == 0
    in_specs = [pl.BlockSpec((tm, K), lambda i, j: (i, 0))]
    args = [x]
    for c in col_starts:
        in_specs.append(pl.BlockSpec((K, tn), lambda i, j, off=c // tn: (0, j + off)))
        args.append(w)
    if bias is not None:
        in_specs.append(pl.BlockSpec((1, tn), lambda i, j: (0, j)))
        args.append(bias)
    out_shape = [jax.ShapeDtypeStruct((M, n), dt) for dt in out_dtypes]
    out_specs = [pl.BlockSpec((tm, tn), lambda i, j: (i, j)) for _ in out_dtypes]
    if head_dim:
        out_shape[0] = jax.ShapeDtypeStruct((M, n // head_dim, head_dim), out_dtypes[0])
        out_specs[0] = pl.BlockSpec((tm, n // head_dim, head_dim), lambda i, j: (i, 0, 0))
    outs = pl.pallas_call(
        functools.partial(_mm_body, n_w=len(col_starts), epilogue=epilogue, has_bias=bias is not None),
        grid=(M // tm, n // tn), in_specs=in_specs, out_specs=out_specs, out_shape=out_shape,
        compiler_params=_cparams("parallel", "parallel"), name=name)(*args)
    return outs


def _cumsum_body(f_ref, crow_ref, ccol_ref, *, chunk):
    T = f_ref.shape[0]
    r = lax.broadcasted_iota(jnp.int32, (chunk, chunk), 0)
    c = lax.broadcasted_iota(jnp.int32, (chunk, chunk), 1)
    tri = jnp.where(c <= r, 1.0, 0.0).astype(BF16)
    carry = jnp.zeros((1, f_ref.shape[1]), F32)
    for ci in range(T // chunk):
        hi, mid, lo = _split3(f_ref[ci * chunk:(ci + 1) * chunk, :])
        cs = _dot(tri, hi) + _dot(tri, mid) + _dot(tri, lo) + carry
        ccol_ref[ci * chunk:(ci + 1) * chunk, :] = cs
        carry = cs[chunk - 1:chunk, :]
    crow_ref[...] = ccol_ref[...].T


def _cumsum(logf_pad, B, T):
    chunk = _tile(T, 256, LANES)
    return pl.pallas_call(
        functools.partial(_cumsum_body, chunk=chunk),
        grid=(B,),
        in_specs=[pl.BlockSpec((T, LANES), lambda b: (b, 0))],
        out_specs=pl.BlockSpec((None, LANES, T), lambda b: (b, 0, 0)),
        out_shape=jax.ShapeDtypeStruct((B, LANES, T), F32),
        scratch_shapes=[pltpu.VMEM((T, LANES), F32)],
        compiler_params=_cparams("parallel"), name="logf_cumsum")(logf_pad)


def _conv_body(prev_ref, cur_ref, w_ref, bdw_ref, g_ref, b_ref, o_ref, sh_ref, h_ref, *, conv_rows, norm_rows, copy_rows):
    i = pl.program_id(1)
    tt, C = cur_ref.shape
    W = w_ref.shape[0]
    sh_ref[0, 0:CONV_HALO, :] = jnp.where(i > 0, prev_ref[...], 0.0)
    sh_ref[0, CONV_HALO:, :] = cur_ref[...]
    n_shifted = tt + CONV_HALO - SUBLANES
    for s in range(1, SUBLANES):
        for r in range(0, n_shifted, copy_rows):
            sh_ref[s, r:r + copy_rows, :] = sh_ref[0, r + s:r + s + copy_rows, :]
    first = CONV_HALO - (W - 1)

    for c in range(C // LANES):
        cs = slice(c * LANES, (c + 1) * LANES)
        taps = [w_ref[j:j + 1, cs] for j in range(W)]

        def conv_rows_fn(r, carry):
            r0 = pl.multiple_of(r * conv_rows, conv_rows)
            accs = [None, None]
            for j in range(W):
                e = first + j
                t = taps[j] * sh_ref[e % SUBLANES, pl.ds(r0 + (e - e % SUBLANES), conv_rows), cs]
                accs[j % 2] = t if accs[j % 2] is None else accs[j % 2] + t
            h_ref[pl.ds(r0, conv_rows), cs] = accs[0] + accs[1]
            return carry

        lax.fori_loop(0, tt // conv_rows, conv_rows_fn, 0)

    def norm_rows_fn(r, carry):
        r0 = pl.multiple_of(r * norm_rows, norm_rows)
        y = _layer_norm(h_ref[pl.ds(r0, norm_rows), :] + bdw_ref[...], g_ref[...], b_ref[...])
        o_ref[pl.ds(r0, norm_rows), :] = (y * jax.nn.sigmoid(y)).astype(o_ref.dtype)
        return carry

    lax.fori_loop(0, tt // norm_rows, norm_rows_fn, 0)


def _conv_prompt(u, w_dw, b_dw, g, b, B, T):
    C = u.shape[1]
    W = w_dw.shape[0]
    assert W - 1 <= CONV_HALO and T % CONV_HALO == 0
    tt = _tile(T, 256, CONV_HALO)
    nt = T // tt
    copy_rows = _tile(tt + CONV_HALO - SUBLANES, 64, SUBLANES)
    return pl.pallas_call(
        functools.partial(_conv_body, conv_rows=_tile(tt, 64, SUBLANES), norm_rows=_tile(tt, 128, 16),
                          copy_rows=copy_rows),
        grid=(B, nt),
        in_specs=[pl.BlockSpec((CONV_HALO, C), lambda bi, i: (jnp.maximum((bi * T + i * tt) // CONV_HALO - 1, 0), 0)),
                  pl.BlockSpec((tt, C), lambda bi, i: (bi * nt + i, 0)),
                  pl.BlockSpec((W, C), lambda bi, i: (0, 0)),
                  pl.BlockSpec((1, C), lambda bi, i: (0, 0)),
                  pl.BlockSpec((1, C), lambda bi, i: (0, 0)),
                  pl.BlockSpec((1, C), lambda bi, i: (0, 0))],
        out_specs=pl.BlockSpec((tt, C), lambda bi, i: (bi * nt + i, 0)),
        out_shape=jax.ShapeDtypeStruct((B * T, C), BF16),
        scratch_shapes=[pltpu.VMEM((SUBLANES, CONV_HALO + tt, C), F32), pltpu.VMEM((tt, C), F32)],
        compiler_params=_cparams("parallel", "parallel"), name="conv_prompt")(u, u, w_dw, b_dw, g, b)


def _fox_body(q_ref, k_ref, v_ref, crow_ref, o_ref, *, H, Dh, scale):
    qi = pl.program_id(1)
    tq = q_ref.shape[0]
    row = lax.broadcasted_iota(jnp.int32, (tq, tq), 0)
    col = lax.broadcasted_iota(jnp.int32, (tq, tq), 1)
    causal = row >= col
    reps = tq // LANES

    def key_block(j, states, masked):
        k0 = pl.multiple_of(j * tq, tq)
        new_states = []
        for h in range(H):
            sl = slice(h * Dh, (h + 1) * Dh)
            m, l, acc = states[h]
            s = _dot_nt(q_ref[:, sl], k_ref[pl.ds(k0, tq), sl]) * scale - crow_ref[h:h + 1, pl.ds(k0, tq)]
            if masked:
                s = jnp.where(causal, s, -jnp.inf)
            m_new = jnp.maximum(m, jnp.broadcast_to(jnp.max(s, axis=-1, keepdims=True), (tq, LANES)))
            a = jnp.exp(m - m_new)
            p = jnp.exp(s - jnp.concatenate([m_new] * reps, axis=1))
            l = a * l + functools.reduce(jnp.add, [p[:, c * LANES:(c + 1) * LANES] for c in range(reps)])
            acc = a * acc + _dot(p.astype(BF16), v_ref[pl.ds(k0, tq), sl])
            new_states.append((m_new, l, acc))
        return tuple(new_states)

    init = tuple((jnp.full((tq, LANES), -jnp.inf, F32), jnp.zeros((tq, LANES), F32), jnp.zeros((tq, Dh), F32))
                 for _ in range(H))
    states = lax.fori_loop(0, qi, functools.partial(key_block, masked=False), init)
    states = key_block(qi, states, True)
    for h in range(H):
        _, l, acc = states[h]
        o_ref[:, h * Dh:(h + 1) * Dh] = (acc / jnp.sum(l, axis=-1, keepdims=True)).astype(o_ref.dtype)


def _fox_prompt(q, k, v, crow, B, T, H, Dh):
    DF = H * Dh
    assert Dh == LANES
    tq = _tile(T, 512, LANES)
    nq = T // tq
    return pl.pallas_call(
        functools.partial(_fox_body, H=H, Dh=Dh, scale=Dh ** -0.5),
        grid=(B, nq),
        in_specs=[pl.BlockSpec((tq, DF), lambda b, i: (b * nq + i, 0)),
                  pl.BlockSpec((T, DF), lambda b, i: (b, 0)),
                  pl.BlockSpec((T, DF), lambda b, i: (b, 0)),
                  pl.BlockSpec((None, SUBLANES, T), lambda b, i: (b, 0, 0))],
        out_specs=pl.BlockSpec((tq, DF), lambda b, i: (b * nq + i, 0)),
        out_shape=jax.ShapeDtypeStruct((B * T, DF), BF16),
        compiler_params=_cparams("parallel", "parallel"), name="fox_prompt")(q, k, v, crow)


def _memattn_body(q_ref, k_ref, v_ref, o_ref, *, H, Dh, scale):
    for h in range(H):
        sl = slice(h * Dh, (h + 1) * Dh)
        s = _dot_nt(q_ref[:, sl], k_ref[:, sl]) * scale
        e = jnp.exp(s - jnp.max(s, axis=-1, keepdims=True))
        p = e / jnp.sum(e, axis=-1, keepdims=True)
        o_ref[:, sl] = _dot(p.astype(BF16), v_ref[:, sl]).astype(o_ref.dtype)


def _memattn_prompt(q, mk, mv, B, T, H, Dh):
    DM = H * Dh
    N = mk.shape[0] // B
    tq = _tile(T, 512, LANES)
    nq = T // tq
    return pl.pallas_call(
        functools.partial(_memattn_body, H=H, Dh=Dh, scale=Dh ** -0.5),
        grid=(B, nq),
        in_specs=[pl.BlockSpec((tq, DM), lambda b, i: (b * nq + i, 0)),
                  pl.BlockSpec((N, DM), lambda b, i: (b, 0)),
                  pl.BlockSpec((N, DM), lambda b, i: (b, 0))],
        out_specs=pl.BlockSpec((tq, DM), lambda b, i: (b * nq + i, 0)),
        out_shape=jax.ShapeDtypeStruct((B * T, DM), BF16),
        compiler_params=_cparams("parallel", "parallel"), name="memattn_prompt")(q, mk, mv)


def _mix_body(x_ref, *refs, nb):
    h_refs, w_refs, wg_refs, bg_refs = (refs[i * nb:(i + 1) * nb] for i in range(4))
    o_ref = refs[4 * nb]
    x = x_ref[...]
    mix = None
    for h_ref, w_ref, wg_ref, bg_ref in zip(h_refs, w_refs, wg_refs, bg_refs):
        term = jax.nn.sigmoid(_dot(x, wg_ref[...]) + bg_ref[...]) * _dot(h_ref[...], w_ref[...])
        mix = term if mix is None else mix + term
    o_ref[...] = mix.astype(o_ref.dtype)


def _mix(x16, branches, w_outs, w_tail, gate_col0, b_gate, *, tm_pref=512, tn_pref=512):
    M, D = x16.shape
    nb = len(branches)
    tm = _tile(M, tm_pref, 16)
    tn = _tile(D, tn_pref, LANES)
    assert gate_col0 % tn == 0
    in_specs = [pl.BlockSpec((tm, D), lambda i, j: (i, 0))]
    in_specs += [pl.BlockSpec((tm, h.shape[1]), lambda i, j: (i, 0)) for h in branches]
    in_specs += [pl.BlockSpec((w.shape[0], tn), lambda i, j: (0, j)) for w in w_outs]
    in_specs += [pl.BlockSpec((D, tn), lambda i, j, off=(gate_col0 + b * D) // tn: (0, j + off)) for b in range(nb)]
    in_specs += [pl.BlockSpec((1, tn), lambda i, j, off=(b * D) // tn: (0, j + off)) for b in range(nb)]
    return pl.pallas_call(
        functools.partial(_mix_body, nb=nb),
        grid=(M // tm, D // tn), in_specs=in_specs,
        out_specs=pl.BlockSpec((tm, tn), lambda i, j: (i, j)),
        out_shape=jax.ShapeDtypeStruct((M, D), BF16),
        compiler_params=_cparams("parallel", "parallel"), name="gated_mix",
    )(x16, *branches, *w_outs, *([w_tail] * nb), *([b_gate] * nb))


def _wo_ln_body(mix_ref, w_ref, x_ref, g_ref, b_ref, o32_ref, o16_ref, *, alpha):
    y = _layer_norm(alpha * x_ref[...] + _dot(mix_ref[...], w_ref[...]), g_ref[...], b_ref[...])
    o32_ref[...] = y
    o16_ref[...] = y.astype(o16_ref.dtype)


def _wo_ln(mix, w_o, x32, g, b, alpha, *, tm_pref=512):
    M, D = x32.shape
    tm = _tile(M, tm_pref, 16)
    row = lambda i: (i, 0)
    fixed = lambda i: (0, 0)
    return pl.pallas_call(
        functools.partial(_wo_ln_body, alpha=alpha),
        grid=(M // tm,),
        in_specs=[pl.BlockSpec((tm, D), row), pl.BlockSpec((D, D), fixed), pl.BlockSpec((tm, D), row),
                  pl.BlockSpec((1, D), fixed), pl.BlockSpec((1, D), fixed)],
        out_specs=[pl.BlockSpec((tm, D), row), pl.BlockSpec((tm, D), row)],
        out_shape=[jax.ShapeDtypeStruct((M, D), F32), jax.ShapeDtypeStruct((M, D), BF16)],
        compiler_params=_cparams("parallel"), name="wo_ln")(mix, w_o, x32, g, b)


def _mlp_body(x_ref, wu_ref, wd_ref, g_ref, b_ref, o32_ref, o16_ref, xb_ref, *, alpha):
    f = pl.program_id(1)

    @pl.when(f == 0)
    def _():
        xb_ref[...] = x_ref[...].astype(BF16)

    h = jnp.maximum(_dot(xb_ref[...], wu_ref[...]), 0.0)
    y = _dot((h * h).astype(BF16), wd_ref[...])

    @pl.when(f == 0)
    def _():
        o32_ref[...] = y

    @pl.when(f > 0)
    def _():
        o32_ref[...] += y

    @pl.when(f == pl.num_programs(1) - 1)
    def _():
        out = _layer_norm(alpha * x_ref[...] + o32_ref[...], g_ref[...], b_ref[...])
        o32_ref[...] = out
        o16_ref[...] = out.astype(o16_ref.dtype)


def _mlp(x32, w_up, w_down, g, b, alpha, *, tm_pref=512, tf_pref=1024):
    M, D = x32.shape
    DFF = w_up.shape[1]
    tm = _tile(M, tm_pref, 16)
    tf = _tile(DFF, tf_pref, LANES)
    row = lambda i, f: (i, 0)
    fixed = lambda i, f: (0, 0)
    return pl.pallas_call(
        functools.partial(_mlp_body, alpha=alpha),
        grid=(M // tm, DFF // tf),
        in_specs=[pl.BlockSpec((tm, D), row), pl.BlockSpec((D, tf), lambda i, f: (0, f)),
                  pl.BlockSpec((tf, D), lambda i, f: (f, 0)),
                  pl.BlockSpec((1, D), fixed), pl.BlockSpec((1, D), fixed)],
        out_specs=[pl.BlockSpec((tm, D), row), pl.BlockSpec((tm, D), row)],
        out_shape=[jax.ShapeDtypeStruct((M, D), F32), jax.ShapeDtypeStruct((M, D), BF16)],
        scratch_shapes=[pltpu.VMEM((tm, D), BF16)],
        compiler_params=_cparams("parallel", "arbitrary"), name="mlp_ln")(x32, w_up, w_down, g, b)


def _conv_sample_body(s_ref, u_ref, w_ref, bdw_ref, g_ref, b_ref, h_ref, ns_ref):
    s = s_ref[...]
    u = u_ref[...]
    Wm1 = s.shape[1]
    w = w_ref[...]
    acc = jnp.sum(s * w[None, :Wm1, :], axis=1, keepdims=True) + u * w[None, Wm1:, :]
    y = _layer_norm(acc + bdw_ref[...][None], g_ref[...][None], b_ref[...][None])
    h_ref[...] = y * jax.nn.sigmoid(y)
    ns_ref[:, 0:Wm1 - 1, :] = s[:, 1:, :]
    ns_ref[:, Wm1 - 1:Wm1, :] = u


def _conv_sample(state, l, u, w_dw, b_dw, g, b):
    _, Bs, Wm1, C = state.shape
    bt = _tile(Bs, 16, 1)
    blk = lambda i: (i, 0, 0)
    fixed = lambda i: (0, 0)
    return pl.pallas_call(
        _conv_sample_body,
        grid=(Bs // bt,),
        in_specs=[pl.BlockSpec((None, bt, Wm1, C), lambda i: (l, i, 0, 0)), pl.BlockSpec((bt, 1, C), blk),
                  pl.BlockSpec((Wm1 + 1, C), fixed), pl.BlockSpec((1, C), fixed),
                  pl.BlockSpec((1, C), fixed), pl.BlockSpec((1, C), fixed)],
        out_specs=[pl.BlockSpec((bt, 1, C), blk), pl.BlockSpec((bt, Wm1, C), blk)],
        out_shape=[jax.ShapeDtypeStruct((Bs, 1, C), F32), jax.ShapeDtypeStruct((Bs, Wm1, C), F32)],
        compiler_params=_cparams("parallel"), name="conv_sample")(state, u.reshape(Bs, 1, C), w_dw, b_dw, g, b)


def _scores3(k3, qs):
    return jnp.sum(k3 * qs[None], axis=-1, keepdims=True)


def _online_update(state, s3, v3):
    m, l, acc = state
    m_new = jnp.maximum(m, jnp.max(s3, axis=0))
    a = jnp.exp(m - m_new)
    e3 = jnp.exp(s3 - m_new[None])
    return m_new, a * l + jnp.sum(e3, axis=0), a * acc + jnp.sum(e3 * v3, axis=0)


def _softmax_state(H, Dh):
    return jnp.full((H, 1), -jnp.inf, F32), jnp.zeros((H, 1), F32), jnp.zeros((H, Dh), F32)


def _fox_sample_body(pt_ref, q_ref, kn_ref, vn_ref, fn_ref, *refs, n_pages, scale):
    del pt_ref
    k_refs, v_refs, f_refs = (refs[i * n_pages:(i + 1) * n_pages] for i in range(3))
    o_ref = refs[3 * n_pages]
    H, Dh = q_ref.shape
    PS = k_refs[0].shape[0]
    assert Dh == LANES
    qs = q_ref[...] * scale
    later = lax.broadcasted_iota(jnp.int32, (PS, H, PS), 2) > lax.broadcasted_iota(jnp.int32, (PS, H, PS), 0)
    ones = jnp.ones((Dh + 2 * PS, LANES), BF16)
    after = jnp.broadcast_to(fn_ref[...], (H, LANES))
    tops, sums, vals = [], [], []
    for p in range(n_pages - 1, -1, -1):
        ft = f_refs[p][...]
        hi = ft.astype(BF16).astype(F32)
        terms = [k_refs[p][...] * qs[None]] + [jnp.where(later, x[None], 0.0) for x in (hi, ft - hi)]
        lhs = jnp.concatenate([t.reshape(PS * H, t.shape[2]).astype(BF16) for t in terms], axis=1)
        raw = _dot(lhs, ones).reshape(PS, H, LANES)
        top = jnp.max(raw, axis=0)
        e3 = jnp.exp(raw - top[None])
        tops.append(top + after)
        sums.append(jnp.sum(e3, axis=0))
        vals.append(jnp.sum(e3 * v_refs[p][...], axis=0))
        after = after + jnp.sum(ft, axis=-1, keepdims=True)
    s_new = jnp.broadcast_to(jnp.sum(kn_ref[...] * qs, axis=-1, keepdims=True), (H, LANES))
    m = functools.reduce(jnp.maximum, tops + [s_new])
    e_new = jnp.exp(s_new - m)
    l = e_new
    acc = e_new * vn_ref[...]
    for top, psum, pval in zip(tops, sums, vals):
        w = jnp.exp(top - m)
        l = l + w * psum
        acc = acc + w * pval
    o_ref[...] = acc / l


def _fox_sample(q, k_new, v_new, logf_new, cache_k, cache_v, cache_logf_t, l, page_table):
    Bs, H, Dh = q.shape
    PS = cache_k.shape[2]
    n_pages = page_table.shape[1]
    one = lambda width: pl.BlockSpec((None, H, width), lambda b, pt: (b, 0, 0))
    kv_page = lambda p: pl.BlockSpec((None, None, PS, H, Dh), lambda b, pt: (l, pt[b, p], 0, 0, 0))
    f_page = lambda p: pl.BlockSpec((None, None, H, PS), lambda b, pt: (l, pt[b, p], 0, 0))
    in_specs = [one(Dh), one(Dh), one(Dh), one(1)]
    in_specs += [kv_page(p) for p in range(n_pages)] + [kv_page(p) for p in range(n_pages)]
    in_specs += [f_page(p) for p in range(n_pages)]
    return pl.pallas_call(
        functools.partial(_fox_sample_body, n_pages=n_pages, scale=Dh ** -0.5),
        grid_spec=pltpu.PrefetchScalarGridSpec(
            num_scalar_prefetch=1, grid=(Bs,), in_specs=in_specs,
            out_specs=pl.BlockSpec((None, H, Dh), lambda b, pt: (b, 0, 0))),
        out_shape=jax.ShapeDtypeStruct((Bs, H, Dh), F32),
        compiler_params=_cparams("arbitrary"), name="fox_sample",
    )(page_table, q, k_new, v_new, logf_new,
      *([cache_k] * n_pages), *([cache_v] * n_pages), *([cache_logf_t] * n_pages))


def _mem_sample_body(q_ref, k_ref, v_ref, o_ref, *, scale):
    G, H, Dh = q_ref.shape
    for g in range(G):
        qs = q_ref[g] * scale
        _, l, acc = _online_update(_softmax_state(H, Dh), _scores3(k_ref[g], qs), v_ref[g])
        o_ref[g] = acc / l


def _mem_sample(q, mem_k, mem_v, l):
    _, Bs, N, H, Dh = mem_k.shape
    G = _tile(Bs, 4, 1)
    blk = lambda i: (i, 0, 0)
    cache = pl.BlockSpec((None, G, N, H, Dh), lambda i: (l, i, 0, 0, 0))
    return pl.pallas_call(
        functools.partial(_mem_sample_body, scale=Dh ** -0.5),
        grid=(Bs // G,),
        in_specs=[pl.BlockSpec((G, H, Dh), blk), cache, cache],
        out_specs=pl.BlockSpec((G, H, Dh), blk),
        out_shape=jax.ShapeDtypeStruct((Bs, H, Dh), F32),
        compiler_params=_cparams("parallel"), name="mem_sample")(q, mem_k, mem_v)


def kernel(x_prompt, x_sample, mem_prompt, cache_k, cache_v, cache_logf, page_table, state_conv, cache_mem_k, cache_mem_v, w_in, b_f, b_gate, w_dw, b_dw, conv_ln_g, conv_ln_b, w_conv_out, w_fox_out, w_mem_kv, w_mem_out, w_o, ln1_g, ln1_b, w_up, w_down, ln2_g, ln2_b):
    depth = w_in.shape[0]
    B, T, D = x_prompt.shape
    Bs, Ts, _ = x_sample.shape
    assert Ts == 1
    W, C = w_dw.shape[1:]
    H, Dh = cache_k.shape[3:]
    DF = H * Dh
    NM, Hm, Dhm = cache_mem_k.shape[2:]
    DM = Hm * Dhm
    n_pool, PS = cache_k.shape[1:3]
    NB = b_gate.shape[1] // D
    alpha = float((2 * depth) ** 0.25)

    c_q, c_k, c_v, c_f = 2 * C, 2 * C + DF, 2 * C + 2 * DF, 2 * C + 3 * DF
    c_tail = c_f + H

    w_main = w_in[:, :, :c_f].astype(BF16)
    w_f = jnp.pad(w_in[:, :, c_f:c_tail], ((0, 0), (0, 0), (0, LANES - H))).astype(BF16)
    w_tail = w_in[:, :, c_tail:].astype(BF16)
    b_f_pad = jnp.pad(b_f, ((0, 0), (0, LANES - H)))[:, None, :]
    w_co, w_fo, w_mo = w_conv_out.astype(BF16), w_fox_out.astype(BF16), w_mem_out.astype(BF16)
    w_mkv, w_o16 = w_mem_kv.astype(BF16), w_o.astype(BF16)
    w_up16, w_down16 = w_up.astype(BF16), w_down.astype(BF16)
    row = lambda a, l: a[l][None, :]

    def project(x16, l, name):
        u, = _mm(x16, w_main[l], [0, C], C, [F32], epilogue="glu", name=name + "_glu")
        q16, = _mm(x16, w_main[l], [c_q], DF, [BF16], name=name + "_q")
        k32, k16 = _mm(x16, w_main[l], [c_k], DF, [F32, BF16], head_dim=Dh, name=name + "_k")
        v32, v16 = _mm(x16, w_main[l], [c_v], DF, [F32, BF16], head_dim=Dh, name=name + "_v")
        logf_pad, = _mm(x16, w_f[l], [0], LANES, [F32], epilogue="logsig", bias=b_f_pad[l], name=name + "_logf")
        qm16, = _mm(x16, w_tail[l], [0], DM, [BF16], name=name + "_qm")
        return u, q16, k32, k16, v32, v16, logf_pad, qm16

    def merge(x32, x16, branches, l):
        mix = _mix(x16, branches, [w_co[l], w_fo[l], w_mo[l]], w_tail[l], DM, row(b_gate, l))
        x1_32, _ = _wo_ln(mix, w_o16[l], x32, row(ln1_g, l), row(ln1_b, l), alpha)
        return _mlp(x1_32, w_up16[l], w_down16[l], row(ln2_g, l), row(ln2_b, l), alpha)

    xp32 = x_prompt.reshape(B * T, D)
    xp16 = xp32.astype(BF16)
    xs32 = x_sample.reshape(Bs, D)
    xs16 = xs32.astype(BF16)
    mem16 = mem_prompt.reshape(B * NM, D).astype(BF16)
    cache_logf_t = jnp.swapaxes(cache_logf, 2, 3)
    outs = [[] for _ in range(10)]
    for l in range(depth):
        conv_args = (w_dw[l], row(b_dw, l), row(conv_ln_g, l), row(conv_ln_b, l))

        u, q16, k32, k16, v32, v16, logf_pad, qm16 = project(xp16, l, "prompt")
        hc = _conv_prompt(u, *conv_args, B, T)
        crow = _cumsum(logf_pad, B, T)
        fa = _fox_prompt(q16, k16, v16, crow, B, T, H, Dh)
        mk32, mk16 = _mm(mem16, w_mkv[l], [0], DM, [F32, BF16], name="mem_k")
        mv32, mv16 = _mm(mem16, w_mkv[l], [DM], DM, [F32, BF16], name="mem_v")
        ma = _memattn_prompt(qm16, mk16, mv16, B, T, Hm, Dhm)
        xp32, xp16 = merge(xp32, xp16, [hc, fa, ma], l)
        outs[0].append(k32.reshape(B, T, H, Dh))
        outs[1].append(v32.reshape(B, T, H, Dh))
        outs[2].append(logf_pad[:, :H].reshape(B, T, H))
        outs[3].append(u.reshape(B, T, C)[:, T - (W - 1):])
        outs[4].append(mk32.reshape(B, NM, Hm, Dhm))
        outs[5].append(mv32.reshape(B, NM, Hm, Dhm))

        u, q16, k32, k16, v32, v16, logf_pad, qm16 = project(xs16, l, "sample")
        hc3, new_state = _conv_sample(state_conv, l, u, *conv_args)
        heads = lambda a: a.astype(F32).reshape(Bs, H, Dh)
        fa = _fox_sample(heads(q16), k32, v32, logf_pad[:, :H, None], cache_k, cache_v, cache_logf_t,
                         l, page_table)
        ma = _mem_sample(qm16.astype(F32).reshape(Bs, Hm, Dhm), cache_mem_k, cache_mem_v, l)
        branches = [hc3.reshape(Bs, C), fa.reshape(Bs, DF), ma.reshape(Bs, DM)]
        xs32, xs16 = merge(xs32, xs16, [a.astype(BF16) for a in branches], l)
        outs[6].append(k32.reshape(Bs, Ts, H, Dh))
        outs[7].append(v32.reshape(Bs, Ts, H, Dh))
        outs[8].append(logf_pad[:, :H].reshape(Bs, Ts, H))
        outs[9].append(new_state)

    return (xp32.reshape(B, T, D), xs32.reshape(Bs, Ts, D)) + tuple(jnp.stack(o) for o in outs)
```

```python
import functools

import jax
import jax.numpy as jnp
from jax import lax
from jax.experimental import pallas as pl
from jax.experimental.pallas import tpu as pltpu

F32 = jnp.float32
BF16 = jnp.bfloat16

LN_EPS = 1e-5
LANES = 128
SUBLANES = 8
VMEM_LIMIT_BYTES = 56 * 1024 * 1024
CONV_HALO = 32


def _cparams(*sem):
    return pltpu.CompilerParams(dimension_semantics=sem, vmem_limit_bytes=VMEM_LIMIT_BYTES)


def _tile(n, pref, mult):
    t = min(pref, n)
    t -= t % mult
    while t >= mult:
        if n % t == 0:
            return t
        t -= mult
    return n


def _layer_norm(r, g, b):
    mu = jnp.mean(r, axis=-1, keepdims=True)
    d = r - mu
    var = jnp.mean(d * d, axis=-1, keepdims=True)
    return d * lax.rsqrt(var + LN_EPS) * g + b


def _split3(f):
    hi = f.astype(BF16)
    r1 = f - hi.astype(F32)
    mid = r1.astype(BF16)
    lo = (r1 - mid.astype(F32)).astype(BF16)
    return hi, mid, lo


def _dot(a, b):
    return jnp.dot(a, b, preferred_element_type=F32)


def _dot_nt(a, b):
    return lax.dot_general(a, b, (((1,), (1,)), ((), ())), preferred_element_type=F32)


def _mm_body(*refs, n_w, epilogue, has_bias, has_carry):
    x_ref = refs[0]
    w_refs = refs[1:1 + n_w]
    pos = 1 + n_w
    b_ref = refs[pos] if has_bias else None
    o_refs = refs[pos + int(has_bias) + int(has_carry):]
    x = x_ref[...]
    zs = [_dot(x, w[...]) for w in w_refs]
    if epilogue == "glu":
        y = zs[0] * jax.nn.sigmoid(zs[1])
    elif epilogue == "logsig":
        y = jax.nn.log_sigmoid(zs[0] + b_ref[...])
    else:
        y = zs[0]
    for o in o_refs:
        if len(o.shape) == 3:
            Dh = o.shape[2]
            for h in range(o.shape[1]):
                o[:, h, :] = y[:, h * Dh:(h + 1) * Dh].astype(o.dtype)
        else:
            o[...] = y.astype(o.dtype)


def _mm(x, w, l, col_starts, n, out_dtypes, *, epilogue="none", bias=None, head_dim=None, per_layer=None,
        tm_pref=1024, tn_pref=1024, name="mm"):
    M, K = x.shape
    tm = _tile(M, tm_pref, 16)
    tn = n if head_dim else _tile(n, tn_pref, LANES)
    for c in col_starts:
        assert c % tn == 0
    in_specs = [pl.BlockSpec((tm, K), lambda i, j: (i, 0))]
    args = [x]
    for c in col_starts:
        in_specs.append(pl.BlockSpec((None, K, tn), lambda i, j, off=c // tn: (l, 0, j + off)))
        args.append(w)
    if bias is not None:
        in_specs.append(pl.BlockSpec((None, 1, tn), lambda i, j: (l, 0, j)))
        args.append(bias)
    out_shape = [jax.ShapeDtypeStruct((M, n), dt) for dt in out_dtypes]
    out_specs = [pl.BlockSpec((tm, tn), lambda i, j: (i, j)) for _ in out_dtypes]
    aliases = {}
    if head_dim:
        out_shape[0] = jax.ShapeDtypeStruct((M, n // head_dim, head_dim), out_dtypes[0])
        out_specs[0] = pl.BlockSpec((tm, n // head_dim, head_dim), lambda i, j: (i, 0, 0))
    if per_layer is not None:
        depth, carried = per_layer
        out_shape[0] = jax.ShapeDtypeStruct((depth, M, n // head_dim, head_dim), out_dtypes[0])
        out_specs[0] = pl.BlockSpec((None, tm, n // head_dim, head_dim), lambda i, j: (l, i, 0, 0))
        if carried is not None:
            aliases = {len(args): 0}
            in_specs.append(pl.BlockSpec(memory_space=pl.ANY))
            args.append(carried)
    outs = pl.pallas_call(
        functools.partial(_mm_body, n_w=len(col_starts), epilogue=epilogue, has_bias=bias is not None,
                          has_carry=bool(aliases)),
        grid=(M // tm, n // tn), in_specs=in_specs, out_specs=out_specs, out_shape=out_shape,
        input_output_aliases=aliases, compiler_params=_cparams("parallel", "parallel"), name=name)(*args)
    return outs


def _cumsum_body(f_ref, crow_ref, ccol_ref, *, chunk):
    T = f_ref.shape[0]
    r = lax.broadcasted_iota(jnp.int32, (chunk, chunk), 0)
    c = lax.broadcasted_iota(jnp.int32, (chunk, chunk), 1)
    tri = jnp.where(c <= r, 1.0, 0.0).astype(BF16)
    carry = jnp.zeros((1, f_ref.shape[1]), F32)
    for ci in range(T // chunk):
        hi, mid, lo = _split3(f_ref[ci * chunk:(ci + 1) * chunk, :])
        cs = _dot(tri, hi) + _dot(tri, mid) + _dot(tri, lo) + carry
        ccol_ref[ci * chunk:(ci + 1) * chunk, :] = cs
        carry = cs[chunk - 1:chunk, :]
    crow_ref[...] = ccol_ref[...].T


def _cumsum(logf_pad, B, T):
    chunk = _tile(T, 256, LANES)
    return pl.pallas_call(
        functools.partial(_cumsum_body, chunk=chunk),
        grid=(B,),
        in_specs=[pl.BlockSpec((T, LANES), lambda b: (b, 0))],
        out_specs=pl.BlockSpec((None, LANES, T), lambda b: (b, 0, 0)),
        out_shape=jax.ShapeDtypeStruct((B, LANES, T), F32),
        scratch_shapes=[pltpu.VMEM((T, LANES), F32)],
        compiler_params=_cparams("parallel"), name="logf_cumsum")(logf_pad)


def _conv_body(prev_ref, cur_ref, w_ref, bdw_ref, g_ref, b_ref, o_ref, sh_ref, h_ref, *, conv_rows, norm_rows, copy_rows):
    i = pl.program_id(1)
    tt, C = cur_ref.shape
    W = w_ref.shape[0]
    groups = [slice(c * LANES, (c + 1) * LANES) for c in range(C // LANES)]
    n_shifted = tt + CONV_HALO - SUBLANES
    first = CONV_HALO - (W - 1)

    for c, cs in enumerate(groups):
        sh_ref[c, 0, 0:CONV_HALO, :] = jnp.where(i > 0, prev_ref[:, cs], 0.0)
        sh_ref[c, 0, CONV_HALO:, :] = cur_ref[:, cs]
        for s in range(1, SUBLANES):
            for r in range(0, n_shifted, copy_rows):
                sh_ref[c, s, r:r + copy_rows, :] = sh_ref[c, 0, r + s:r + s + copy_rows, :]
        taps = [w_ref[j:j + 1, cs] for j in range(W)]

        def conv_rows_fn(r, carry):
            r0 = pl.multiple_of(r * conv_rows, conv_rows)
            accs = [None, None]
            for j in range(W):
                e = first + j
                t = taps[j] * sh_ref[c, e % SUBLANES, pl.ds(r0 + (e - e % SUBLANES), conv_rows), :]
                accs[j % 2] = t if accs[j % 2] is None else accs[j % 2] + t
            h_ref[c, pl.ds(r0, conv_rows), :] = accs[0] + accs[1]
            return carry

        lax.fori_loop(0, tt // conv_rows, conv_rows_fn, 0)

    def norm_rows_fn(r, carry):
        r0 = pl.multiple_of(r * norm_rows, norm_rows)
        xs = [h_ref[c, pl.ds(r0, norm_rows), :] + bdw_ref[:, cs] for c, cs in enumerate(groups)]
        mu = jnp.sum(functools.reduce(jnp.add, xs), axis=-1, keepdims=True) / C
        ds = [x - mu for x in xs]
        var = jnp.sum(functools.reduce(jnp.add, [d * d for d in ds]), axis=-1, keepdims=True) / C
        inv = lax.rsqrt(var + LN_EPS)
        for d, cs in zip(ds, groups):
            y = d * inv * g_ref[:, cs] + b_ref[:, cs]
            o_ref[pl.ds(r0, norm_rows), cs] = (y * jax.nn.sigmoid(y)).astype(o_ref.dtype)
        return carry

    lax.fori_loop(0, tt // norm_rows, norm_rows_fn, 0)


def _conv_prompt(u, l, w_dw, b_dw, g, b, B, T):
    C = u.shape[1]
    W = w_dw.shape[1]
    vec = pl.BlockSpec((None, 1, C), lambda bi, i: (l, 0, 0))
    assert W - 1 <= CONV_HALO and T % CONV_HALO == 0
    tt = _tile(T, 256, CONV_HALO)
    nt = T // tt
    copy_rows = _tile(tt + CONV_HALO - SUBLANES, 64, SUBLANES)
    return pl.pallas_call(
        functools.partial(_conv_body, conv_rows=_tile(tt, 64, SUBLANES), norm_rows=_tile(tt, 128, 16),
                          copy_rows=copy_rows),
        grid=(B, nt),
        in_specs=[pl.BlockSpec((CONV_HALO, C), lambda bi, i: (jnp.maximum((bi * T + i * tt) // CONV_HALO - 1, 0), 0)),
                  pl.BlockSpec((tt, C), lambda bi, i: (bi * nt + i, 0)),
                  pl.BlockSpec((None, W, C), lambda bi, i: (l, 0, 0)), vec, vec, vec],
        out_specs=pl.BlockSpec((tt, C), lambda bi, i: (bi * nt + i, 0)),
        out_shape=jax.ShapeDtypeStruct((B * T, C), BF16),
        scratch_shapes=[pltpu.VMEM((C // LANES, SUBLANES, CONV_HALO + tt, LANES), F32),
                        pltpu.VMEM((C // LANES, tt, LANES), F32)],
        compiler_params=_cparams("parallel", "parallel"), name="conv_prompt")(u, u, w_dw, b_dw, g, b)


def _fox_body(q_ref, k_ref, v_ref, crow_ref, o_ref, *, H, Dh, scale):
    qi = pl.program_id(1)
    tq = q_ref.shape[0]
    row = lax.broadcasted_iota(jnp.int32, (tq, tq), 0)
    col = lax.broadcasted_iota(jnp.int32, (tq, tq), 1)
    causal = row >= col
    reps = tq // LANES

    def key_block(j, states, masked):
        k0 = pl.multiple_of(j * tq, tq)
        new_states = []
        for h in range(H):
            sl = slice(h * Dh, (h + 1) * Dh)
            m, l, acc = states[h]
            s = _dot_nt(q_ref[:, sl], k_ref[pl.ds(k0, tq), sl]) * scale - crow_ref[h:h + 1, pl.ds(k0, tq)]
            if masked:
                s = jnp.where(causal, s, -jnp.inf)
            m_new = jnp.maximum(m, jnp.broadcast_to(jnp.max(s, axis=-1, keepdims=True), (tq, LANES)))
            a = jnp.exp(m - m_new)
            p = jnp.exp(s - jnp.concatenate([m_new] * reps, axis=1))
            l = a * l + functools.reduce(jnp.add, [p[:, c * LANES:(c + 1) * LANES] for c in range(reps)])
            acc = a * acc + _dot(p.astype(BF16), v_ref[pl.ds(k0, tq), sl])
            new_states.append((m_new, l, acc))
        return tuple(new_states)

    init = tuple((jnp.full((tq, LANES), -jnp.inf, F32), jnp.zeros((tq, LANES), F32), jnp.zeros((tq, Dh), F32))
                 for _ in range(H))
    states = lax.fori_loop(0, qi, functools.partial(key_block, masked=False), init)
    states = key_block(qi, states, True)
    for h in range(H):
        _, l, acc = states[h]
        o_ref[:, h * Dh:(h + 1) * Dh] = (acc / jnp.sum(l, axis=-1, keepdims=True)).astype(o_ref.dtype)


def _fox_prompt(q, k, v, crow, B, T, H, Dh):
    DF = H * Dh
    assert Dh == LANES
    tq = _tile(T, 512, LANES)
    nq = T // tq
    return pl.pallas_call(
        functools.partial(_fox_body, H=H, Dh=Dh, scale=Dh ** -0.5),
        grid=(B, nq),
        in_specs=[pl.BlockSpec((tq, DF), lambda b, i: (b * nq + i, 0)),
                  pl.BlockSpec((T, DF), lambda b, i: (b, 0)),
                  pl.BlockSpec((T, DF), lambda b, i: (b, 0)),
                  pl.BlockSpec((None, SUBLANES, T), lambda b, i: (b, 0, 0))],
        out_specs=pl.BlockSpec((tq, DF), lambda b, i: (b * nq + i, 0)),
        out_shape=jax.ShapeDtypeStruct((B * T, DF), BF16),
        compiler_params=_cparams("parallel", "parallel"), name="fox_prompt")(q, k, v, crow)


def _memattn_body(q_ref, k_ref, v_ref, o_ref, *, H, Dh, scale):
    for h in range(H):
        sl = slice(h * Dh, (h + 1) * Dh)
        s = _dot_nt(q_ref[:, sl], k_ref[:, sl]) * scale
        e = jnp.exp(s - jnp.max(s, axis=-1, keepdims=True))
        p = e / jnp.sum(e, axis=-1, keepdims=True)
        o_ref[:, sl] = _dot(p.astype(BF16), v_ref[:, sl]).astype(o_ref.dtype)


def _memattn_prompt(q, mk, mv, B, T, H, Dh):
    DM = H * Dh
    N = mk.shape[0] // B
    tq = _tile(T, 512, LANES)
    nq = T // tq
    return pl.pallas_call(
        functools.partial(_memattn_body, H=H, Dh=Dh, scale=Dh ** -0.5),
        grid=(B, nq),
        in_specs=[pl.BlockSpec((tq, DM), lambda b, i: (b * nq + i, 0)),
                  pl.BlockSpec((N, DM), lambda b, i: (b, 0)),
                  pl.BlockSpec((N, DM), lambda b, i: (b, 0))],
        out_specs=pl.BlockSpec((tq, DM), lambda b, i: (b * nq + i, 0)),
        out_shape=jax.ShapeDtypeStruct((B * T, DM), BF16),
        compiler_params=_cparams("parallel", "parallel"), name="memattn_prompt")(q, mk, mv)


def _mix_body(x_ref, *refs, nb):
    h_refs, w_refs, wg_refs, bg_refs = (refs[i * nb:(i + 1) * nb] for i in range(4))
    o_ref = refs[4 * nb]
    x = x_ref[...]
    mix = None
    for h_ref, w_ref, wg_ref, bg_ref in zip(h_refs, w_refs, wg_refs, bg_refs):
        term = jax.nn.sigmoid(_dot(x, wg_ref[...]) + bg_ref[...]) * _dot(h_ref[...], w_ref[...])
        mix = term if mix is None else mix + term
    o_ref[...] = mix.astype(o_ref.dtype)


def _mix(x16, branches, w_outs, w_tail, l, gate_col0, b_gate, *, tm_pref=512, tn_pref=512):
    M, D = x16.shape
    nb = len(branches)
    tm = _tile(M, tm_pref, 16)
    tn = _tile(D, tn_pref, LANES)
    assert gate_col0 % tn == 0
    in_specs = [pl.BlockSpec((tm, D), lambda i, j: (i, 0))]
    in_specs += [pl.BlockSpec((tm, h.shape[1]), lambda i, j: (i, 0)) for h in branches]
    in_specs += [pl.BlockSpec((None, w.shape[1], tn), lambda i, j: (l, 0, j)) for w in w_outs]
    in_specs += [pl.BlockSpec((None, D, tn), lambda i, j, off=(gate_col0 + b * D) // tn: (l, 0, j + off))
                 for b in range(nb)]
    in_specs += [pl.BlockSpec((None, 1, tn), lambda i, j, off=(b * D) // tn: (l, 0, j + off)) for b in range(nb)]
    return pl.pallas_call(
        functools.partial(_mix_body, nb=nb),
        grid=(M // tm, D // tn), in_specs=in_specs,
        out_specs=pl.BlockSpec((tm, tn), lambda i, j: (i, j)),
        out_shape=jax.ShapeDtypeStruct((M, D), BF16),
        compiler_params=_cparams("parallel", "parallel"), name="gated_mix",
    )(x16, *branches, *w_outs, *([w_tail] * nb), *([b_gate] * nb))


def _wo_ln_body(mix_ref, w_ref, x_ref, g_ref, b_ref, o32_ref, o16_ref, *, alpha):
    y = _layer_norm(alpha * x_ref[...] + _dot(mix_ref[...], w_ref[...]), g_ref[...], b_ref[...])
    o32_ref[...] = y
    o16_ref[...] = y.astype(o16_ref.dtype)


def _wo_ln(mix, w_o, l, x32, g, b, alpha, *, tm_pref=512):
    M, D = x32.shape
    tm = _tile(M, tm_pref, 16)
    row = lambda i: (i, 0)
    layer = lambda i: (l, 0, 0)
    return pl.pallas_call(
        functools.partial(_wo_ln_body, alpha=alpha),
        grid=(M // tm,),
        in_specs=[pl.BlockSpec((tm, D), row), pl.BlockSpec((None, D, D), layer), pl.BlockSpec((tm, D), row),
                  pl.BlockSpec((None, 1, D), layer), pl.BlockSpec((None, 1, D), layer)],
        out_specs=[pl.BlockSpec((tm, D), row), pl.BlockSpec((tm, D), row)],
        out_shape=[jax.ShapeDtypeStruct((M, D), F32), jax.ShapeDtypeStruct((M, D), BF16)],
        compiler_params=_cparams("parallel"), name="wo_ln")(mix, w_o, x32, g, b)


def _mlp_body(x_ref, wu_ref, wd_ref, g_ref, b_ref, o32_ref, o16_ref, xb_ref, *, alpha):
    f = pl.program_id(1)

    @pl.when(f == 0)
    def _():
        xb_ref[...] = x_ref[...].astype(BF16)

    h = jnp.maximum(_dot(xb_ref[...], wu_ref[...]), 0.0)
    y = _dot((h * h).astype(BF16), wd_ref[...])

    @pl.when(f == 0)
    def _():
        o32_ref[...] = y

    @pl.when(f > 0)
    def _():
        o32_ref[...] += y

    @pl.when(f == pl.num_programs(1) - 1)
    def _():
        out = _layer_norm(alpha * x_ref[...] + o32_ref[...], g_ref[...], b_ref[...])
        o32_ref[...] = out
        o16_ref[...] = out.astype(o16_ref.dtype)


def _mlp(x32, w_up, w_down, l, g, b, alpha, *, tm_pref=512, tf_pref=1024):
    M, D = x32.shape
    DFF = w_up.shape[2]
    tm = _tile(M, tm_pref, 16)
    tf = _tile(DFF, tf_pref, LANES)
    row = lambda i, f: (i, 0)
    layer = lambda i, f: (l, 0, 0)
    return pl.pallas_call(
        functools.partial(_mlp_body, alpha=alpha),
        grid=(M // tm, DFF // tf),
        in_specs=[pl.BlockSpec((tm, D), row), pl.BlockSpec((None, D, tf), lambda i, f: (l, 0, f)),
                  pl.BlockSpec((None, tf, D), lambda i, f: (l, f, 0)),
                  pl.BlockSpec((None, 1, D), layer), pl.BlockSpec((None, 1, D), layer)],
        out_specs=[pl.BlockSpec((tm, D), row), pl.BlockSpec((tm, D), row)],
        out_shape=[jax.ShapeDtypeStruct((M, D), F32), jax.ShapeDtypeStruct((M, D), BF16)],
        scratch_shapes=[pltpu.VMEM((tm, D), BF16)],
        compiler_params=_cparams("parallel", "arbitrary"), name="mlp_ln")(x32, w_up, w_down, g, b)


def _conv_sample_body(s_ref, u_ref, w_ref, bdw_ref, g_ref, b_ref, h_ref, ns_ref):
    s = s_ref[...]
    u = u_ref[...]
    Wm1 = s.shape[1]
    w = w_ref[...]
    acc = jnp.sum(s * w[None, :Wm1, :], axis=1, keepdims=True) + u * w[None, Wm1:, :]
    y = _layer_norm(acc + bdw_ref[...][None], g_ref[...][None], b_ref[...][None])
    h_ref[...] = y * jax.nn.sigmoid(y)
    ns_ref[:, 0:Wm1 - 1, :] = s[:, 1:, :]
    ns_ref[:, Wm1 - 1:Wm1, :] = u


def _conv_sample(state, l, u, w_dw, b_dw, g, b):
    _, Bs, Wm1, C = state.shape
    bt = _tile(Bs, 16, 1)
    blk = lambda i: (i, 0, 0)
    vec = pl.BlockSpec((None, 1, C), lambda i: (l, 0, 0))
    return pl.pallas_call(
        _conv_sample_body,
        grid=(Bs // bt,),
        in_specs=[pl.BlockSpec((None, bt, Wm1, C), lambda i: (l, i, 0, 0)), pl.BlockSpec((bt, 1, C), blk),
                  pl.BlockSpec((None, Wm1 + 1, C), lambda i: (l, 0, 0)), vec, vec, vec],
        out_specs=[pl.BlockSpec((bt, 1, C), blk), pl.BlockSpec((bt, Wm1, C), blk)],
        out_shape=[jax.ShapeDtypeStruct((Bs, 1, C), F32), jax.ShapeDtypeStruct((Bs, Wm1, C), F32)],
        compiler_params=_cparams("parallel"), name="conv_sample")(state, u.reshape(Bs, 1, C), w_dw, b_dw, g, b)


def _scores3(k3, qs):
    return jnp.sum(k3 * qs[None], axis=-1, keepdims=True)


def _online_update(state, s3, v3):
    m, l, acc = state
    m_new = jnp.maximum(m, jnp.max(s3, axis=0))
    a = jnp.exp(m - m_new)
    e3 = jnp.exp(s3 - m_new[None])
    return m_new, a * l + jnp.sum(e3, axis=0), a * acc + jnp.sum(e3 * v3, axis=0)


def _softmax_state(H, Dh):
    return jnp.full((H, 1), -jnp.inf, F32), jnp.zeros((H, 1), F32), jnp.zeros((H, Dh), F32)


def _fox_sample_body(pt_ref, q_ref, kn_ref, vn_ref, fn_ref, *refs, n_pages, scale):
    del pt_ref
    k_refs, v_refs, f_refs = (refs[i * n_pages:(i + 1) * n_pages] for i in range(3))
    o_ref = refs[3 * n_pages]
    H, Dh = q_ref.shape
    PS = k_refs[0].shape[0]
    assert Dh == LANES
    qs = q_ref[...] * scale
    later = lax.broadcasted_iota(jnp.int32, (PS, H, PS), 2) > lax.broadcasted_iota(jnp.int32, (PS, H, PS), 0)
    ones = jnp.ones((Dh + 2 * PS, LANES), BF16)
    after = jnp.broadcast_to(fn_ref[...], (H, LANES))
    tops, sums, vals = [], [], []
    for p in range(n_pages - 1, -1, -1):
        ft = f_refs[p][...]
        hi = ft.astype(BF16).astype(F32)
        terms = [k_refs[p][...] * qs[None]] + [jnp.where(later, x[None], 0.0) for x in (hi, ft - hi)]
        lhs = jnp.concatenate([t.reshape(PS * H, t.shape[2]).astype(BF16) for t in terms], axis=1)
        raw = _dot(lhs, ones).reshape(PS, H, LANES)
        top = jnp.max(raw, axis=0)
        e3 = jnp.exp(raw - top[None])
        tops.append(top + after)
        sums.append(jnp.sum(e3, axis=0))
        vals.append(jnp.sum(e3 * v_refs[p][...], axis=0))
        after = after + jnp.sum(ft, axis=-1, keepdims=True)
    s_new = jnp.broadcast_to(jnp.sum(kn_ref[...] * qs, axis=-1, keepdims=True), (H, LANES))
    m = functools.reduce(jnp.maximum, tops + [s_new])
    e_new = jnp.exp(s_new - m)
    l = e_new
    acc = e_new * vn_ref[...]
    for top, psum, pval in zip(tops, sums, vals):
        w = jnp.exp(top - m)
        l = l + w * psum
        acc = acc + w * pval
    o_ref[...] = acc / l


def _fox_sample(q, k_new, v_new, logf_new, cache_k, cache_v, cache_logf_t, l, page_table):
    Bs, H, Dh = q.shape
    PS = cache_k.shape[2]
    n_pages = page_table.shape[1]
    one = lambda width: pl.BlockSpec((None, H, width), lambda b, pt: (b, 0, 0))
    kv_page = lambda p: pl.BlockSpec((None, None, PS, H, Dh), lambda b, pt: (l, pt[b, p], 0, 0, 0))
    f_page = lambda p: pl.BlockSpec((None, None, H, PS), lambda b, pt: (l, pt[b, p], 0, 0))
    in_specs = [one(Dh), one(Dh), one(Dh), one(1)]
    in_specs += [kv_page(p) for p in range(n_pages)] + [kv_page(p) for p in range(n_pages)]
    in_specs += [f_page(p) for p in range(n_pages)]
    return pl.pallas_call(
        functools.partial(_fox_sample_body, n_pages=n_pages, scale=Dh ** -0.5),
        grid_spec=pltpu.PrefetchScalarGridSpec(
            num_scalar_prefetch=1, grid=(Bs,), in_specs=in_specs,
            out_specs=pl.BlockSpec((None, H, Dh), lambda b, pt: (b, 0, 0))),
        out_shape=jax.ShapeDtypeStruct((Bs, H, Dh), F32),
        compiler_params=_cparams("arbitrary"), name="fox_sample",
    )(page_table, q, k_new, v_new, logf_new,
      *([cache_k] * n_pages), *([cache_v] * n_pages), *([cache_logf_t] * n_pages))


def _mem_sample_body(q_ref, k_ref, v_ref, o_ref, *, scale):
    G, H, Dh = q_ref.shape
    for g in range(G):
        qs = q_ref[g] * scale
        _, l, acc = _online_update(_softmax_state(H, Dh), _scores3(k_ref[g], qs), v_ref[g])
        o_ref[g] = acc / l


def _mem_sample(q, mem_k, mem_v, l):
    _, Bs, N, H, Dh = mem_k.shape
    G = _tile(Bs, 4, 1)
    blk = lambda i: (i, 0, 0)
    cache = pl.BlockSpec((None, G, N, H, Dh), lambda i: (l, i, 0, 0, 0))
    return pl.pallas_call(
        functools.partial(_mem_sample_body, scale=Dh ** -0.5),
        grid=(Bs // G,),
        in_specs=[pl.BlockSpec((G, H, Dh), blk), cache, cache],
        out_specs=pl.BlockSpec((G, H, Dh), blk),
        out_shape=jax.ShapeDtypeStruct((Bs, H, Dh), F32),
        compiler_params=_cparams("parallel"), name="mem_sample")(q, mem_k, mem_v)


def kernel(x_prompt, x_sample, mem_prompt, cache_k, cache_v, cache_logf, page_table, state_conv, cache_mem_k, cache_mem_v, w_in, b_f, b_gate, w_dw, b_dw, conv_ln_g, conv_ln_b, w_conv_out, w_fox_out, w_mem_kv, w_mem_out, w_o, ln1_g, ln1_b, w_up, w_down, ln2_g, ln2_b):
    depth = w_in.shape[0]
    B, T, D = x_prompt.shape
    Bs, Ts, _ = x_sample.shape
    assert Ts == 1
    W, C = w_dw.shape[1:]
    H, Dh = cache_k.shape[3:]
    DF = H * Dh
    NM, Hm, Dhm = cache_mem_k.shape[2:]
    DM = Hm * Dhm
    n_pool, PS = cache_k.shape[1:3]
    NB = b_gate.shape[1] // D
    alpha = float((2 * depth) ** 0.25)

    c_q, c_k, c_v, c_f = 2 * C, 2 * C + DF, 2 * C + 2 * DF, 2 * C + 3 * DF
    c_tail = c_f + H

    w_main = w_in[:, :, :c_f].astype(BF16)
    w_f = jnp.pad(w_in[:, :, c_f:c_tail], ((0, 0), (0, 0), (0, LANES - H))).astype(BF16)
    w_tail = w_in[:, :, c_tail:].astype(BF16)
    vec = lambda a: a[:, None, :]
    b_f_pad = vec(jnp.pad(b_f, ((0, 0), (0, LANES - H))))
    w_co, w_fo, w_mo = w_conv_out.astype(BF16), w_fox_out.astype(BF16), w_mem_out.astype(BF16)
    w_mkv, w_o16 = w_mem_kv.astype(BF16), w_o.astype(BF16)
    w_up16, w_down16 = w_up.astype(BF16), w_down.astype(BF16)
    b_gate3, ln1_g3, ln1_b3, ln2_g3, ln2_b3 = vec(b_gate), vec(ln1_g), vec(ln1_b), vec(ln2_g), vec(ln2_b)
    conv_args = (w_dw, vec(b_dw), vec(conv_ln_g), vec(conv_ln_b))

    def project(x16, l, name, kv_carry=None):
        u, = _mm(x16, w_main, l, [0, C], C, [F32], epilogue="glu", name=name + "_glu")
        q16, = _mm(x16, w_main, l, [c_q], DF, [BF16], name=name + "_q")
        per_layer = [None, None] if kv_carry is None else [(depth, c) for c in kv_carry]
        k32, k16 = _mm(x16, w_main, l, [c_k], DF, [F32, BF16], head_dim=Dh, per_layer=per_layer[0], name=name + "_k")
        v32, v16 = _mm(x16, w_main, l, [c_v], DF, [F32, BF16], head_dim=Dh, per_layer=per_layer[1], name=name + "_v")
        logf_pad, = _mm(x16, w_f, l, [0], LANES, [F32], epilogue="logsig", bias=b_f_pad, name=name + "_logf")
        qm16, = _mm(x16, w_tail, l, [0], DM, [BF16], name=name + "_qm")
        return u, q16, k32, k16, v32, v16, logf_pad, qm16

    def merge(x32, x16, branches, l):
        mix = _mix(x16, branches, [w_co, w_fo, w_mo], w_tail, l, DM, b_gate3)
        x1_32, _ = _wo_ln(mix, w_o16, l, x32, ln1_g3, ln1_b3, alpha)
        return _mlp(x1_32, w_up16, w_down16, l, ln2_g3, ln2_b3, alpha)

    xp32 = x_prompt.reshape(B * T, D)
    xp16 = xp32.astype(BF16)
    xs32 = x_sample.reshape(Bs, D)
    xs16 = xs32.astype(BF16)
    mem16 = mem_prompt.reshape(B * NM, D).astype(BF16)
    cache_logf_t = jnp.swapaxes(cache_logf, 2, 3)
    outs = [[] for _ in range(10)]
    k_prompt = v_prompt = None
    for l in range(depth):
        u, q16, k_prompt, k16, v_prompt, v16, logf_pad, qm16 = project(xp16, l, "prompt", (k_prompt, v_prompt))
        hc = _conv_prompt(u, l, *conv_args, B, T)
        crow = _cumsum(logf_pad, B, T)
        fa = _fox_prompt(q16, k16, v16, crow, B, T, H, Dh)
        mk32, mk16 = _mm(mem16, w_mkv, l, [0], DM, [F32, BF16], name="mem_k")
        mv32, mv16 = _mm(mem16, w_mkv, l, [DM], DM, [F32, BF16], name="mem_v")
        ma = _memattn_prompt(qm16, mk16, mv16, B, T, Hm, Dhm)
        xp32, xp16 = merge(xp32, xp16, [hc, fa, ma], l)
        outs[2].append(logf_pad[:, :H].reshape(B, T, H))
        outs[3].append(u.reshape(B, T, C)[:, T - (W - 1):])
        outs[4].append(mk32.reshape(B, NM, Hm, Dhm))
        outs[5].append(mv32.reshape(B, NM, Hm, Dhm))

        u, q16, k32, k16, v32, v16, logf_pad, qm16 = project(xs16, l, "sample")
        hc3, new_state = _conv_sample(state_conv, l, u, *conv_args)
        heads = lambda a: a.astype(F32).reshape(Bs, H, Dh)
        fa = _fox_sample(heads(q16), k32, v32, logf_pad[:, :H, None], cache_k, cache_v, cache_logf_t,
                         l, page_table)
        ma = _mem_sample(qm16.astype(F32).reshape(Bs, Hm, Dhm), cache_mem_k, cache_mem_v, l)
        branches = [hc3.reshape(Bs, C), fa.reshape(Bs, DF), ma.reshape(Bs, DM)]
        xs32, xs16 = merge(xs32, xs16, [a.astype(BF16) for a in branches], l)
        outs[6].append(k32.reshape(Bs, Ts, H, Dh))
        outs[7].append(v32.reshape(Bs, Ts, H, Dh))
        outs[8].append(logf_pad[:, :H].reshape(Bs, Ts, H))
        outs[9].append(new_state)

    stacked = [None if not o else jnp.stack(o) for o in outs]
    stacked[0] = k_prompt.reshape(depth, B, T, H, Dh)
    stacked[1] = v_prompt.reshape(depth, B, T, H, Dh)
    return (xp32.reshape(B, T, D), xs32.reshape(Bs, Ts, D)) + tuple(stacked)
```

```python
import functools

import jax
import jax.numpy as jnp
from jax import lax
from jax.experimental import pallas as pl
from jax.experimental.pallas import tpu as pltpu

F32 = jnp.float32
BF16 = jnp.bfloat16

LN_EPS = 1e-5
LOG2_E = 1.4426950408889634
LANES = 128
SUBLANES = 8
VMEM_LIMIT_BYTES = 56 * 1024 * 1024
CONV_HALO = 32


def _cparams(*sem):
    return pltpu.CompilerParams(dimension_semantics=sem, vmem_limit_bytes=VMEM_LIMIT_BYTES)


def _tile(n, pref, mult):
    t = min(pref, n)
    t -= t % mult
    while t >= mult:
        if n % t == 0:
            return t
        t -= mult
    return n


def _layer_norm(r, g, b):
    mu = jnp.mean(r, axis=-1, keepdims=True)
    d = r - mu
    var = jnp.mean(d * d, axis=-1, keepdims=True)
    return d * lax.rsqrt(var + LN_EPS) * g + b


def _split3(f):
    hi = f.astype(BF16)
    r1 = f - hi.astype(F32)
    mid = r1.astype(BF16)
    lo = (r1 - mid.astype(F32)).astype(BF16)
    return hi, mid, lo


def _dot(a, b):
    return jnp.dot(a, b, preferred_element_type=F32)


def _dot_nt(a, b):
    return lax.dot_general(a, b, (((1,), (1,)), ((), ())), preferred_element_type=F32)


def _mm_body(*refs, n_w, epilogue, has_bias, has_carry):
    x_ref = refs[0]
    w_refs = refs[1:1 + n_w]
    pos = 1 + n_w
    b_ref = refs[pos] if has_bias else None
    o_refs = refs[pos + int(has_bias) + int(has_carry):]
    x = x_ref[...]
    zs = [_dot(x, w[...]) for w in w_refs]
    if epilogue == "glu":
        y = zs[0] * jax.nn.sigmoid(zs[1])
    elif epilogue == "logsig":
        y = jax.nn.log_sigmoid(zs[0] + b_ref[...])
    else:
        y = zs[0]
    for o in o_refs:
        if len(o.shape) == 3:
            Dh = o.shape[2]
            for h in range(o.shape[1]):
                o[:, h, :] = y[:, h * Dh:(h + 1) * Dh].astype(o.dtype)
        else:
            o[...] = y.astype(o.dtype)


def _mm(x, w, l, col_starts, n, out_dtypes, *, epilogue="none", bias=None, head_dim=None, per_layer=None,
        tm_pref=1024, tn_pref=1024, name="mm"):
    M, K = x.shape
    tm = _tile(M, tm_pref, 16)
    tn = n if head_dim else _tile(n, tn_pref, LANES)
    for c in col_starts:
        assert c % tn == 0
    in_specs = [pl.BlockSpec((tm, K), lambda i, j: (i, 0))]
    args = [x]
    for c in col_starts:
        in_specs.append(pl.BlockSpec((None, K, tn), lambda i, j, off=c // tn: (l, 0, j + off)))
        args.append(w)
    if bias is not None:
        in_specs.append(pl.BlockSpec((None, 1, tn), lambda i, j: (l, 0, j)))
        args.append(bias)
    out_shape = [jax.ShapeDtypeStruct((M, n), dt) for dt in out_dtypes]
    out_specs = [pl.BlockSpec((tm, tn), lambda i, j: (i, j)) for _ in out_dtypes]
    aliases = {}
    if head_dim:
        out_shape[0] = jax.ShapeDtypeStruct((M, n // head_dim, head_dim), out_dtypes[0])
        out_specs[0] = pl.BlockSpec((tm, n // head_dim, head_dim), lambda i, j: (i, 0, 0))
    if per_layer is not None:
        depth, carried = per_layer
        out_shape[0] = jax.ShapeDtypeStruct((depth, M, n // head_dim, head_dim), out_dtypes[0])
        out_specs[0] = pl.BlockSpec((None, tm, n // head_dim, head_dim), lambda i, j: (l, i, 0, 0))
        if carried is not None:
            aliases = {len(args): 0}
            in_specs.append(pl.BlockSpec(memory_space=pl.ANY))
            args.append(carried)
    outs = pl.pallas_call(
        functools.partial(_mm_body, n_w=len(col_starts), epilogue=epilogue, has_bias=bias is not None,
                          has_carry=bool(aliases)),
        grid=(M // tm, n // tn), in_specs=in_specs, out_specs=out_specs, out_shape=out_shape,
        input_output_aliases=aliases, compiler_params=_cparams("parallel", "parallel"), name=name)(*args)
    return outs


def _cumsum_body(f_ref, crow_ref, ccol_ref, *, chunk):
    T = f_ref.shape[0]
    r = lax.broadcasted_iota(jnp.int32, (chunk, chunk), 0)
    c = lax.broadcasted_iota(jnp.int32, (chunk, chunk), 1)
    tri = jnp.where(c <= r, 1.0, 0.0).astype(BF16)
    carry = jnp.zeros((1, f_ref.shape[1]), F32)
    for ci in range(T // chunk):
        hi, mid, lo = _split3(f_ref[ci * chunk:(ci + 1) * chunk, :])
        cs = _dot(tri, hi) + _dot(tri, mid) + _dot(tri, lo) + carry
        ccol_ref[ci * chunk:(ci + 1) * chunk, :] = cs
        carry = cs[chunk - 1:chunk, :]
    crow_ref[...] = ccol_ref[...].T


def _cumsum(logf_pad, B, T):
    chunk = _tile(T, 256, LANES)
    return pl.pallas_call(
        functools.partial(_cumsum_body, chunk=chunk),
        grid=(B,),
        in_specs=[pl.BlockSpec((T, LANES), lambda b: (b, 0))],
        out_specs=pl.BlockSpec((None, LANES, T), lambda b: (b, 0, 0)),
        out_shape=jax.ShapeDtypeStruct((B, LANES, T), F32),
        scratch_shapes=[pltpu.VMEM((T, LANES), F32)],
        compiler_params=_cparams("parallel"), name="logf_cumsum")(logf_pad)


def _conv_body(prev_ref, cur_ref, w_ref, bdw_ref, g_ref, b_ref, o_ref, sh_ref, h_ref, *, conv_rows, norm_rows, copy_rows):
    i = pl.program_id(1)
    tt, C = cur_ref.shape
    W = w_ref.shape[0]
    groups = [slice(c * LANES, (c + 1) * LANES) for c in range(C // LANES)]
    n_shifted = tt + CONV_HALO - SUBLANES
    first = CONV_HALO - (W - 1)

    for c, cs in enumerate(groups):
        sh_ref[c, 0, 0:CONV_HALO, :] = jnp.where(i > 0, prev_ref[:, cs], 0.0)
        sh_ref[c, 0, CONV_HALO:, :] = cur_ref[:, cs]
        for s in range(1, SUBLANES):
            for r in range(0, n_shifted, copy_rows):
                sh_ref[c, s, r:r + copy_rows, :] = sh_ref[c, 0, r + s:r + s + copy_rows, :]
        taps = [w_ref[j:j + 1, cs] for j in range(W)]

        def conv_rows_fn(r, carry):
            r0 = pl.multiple_of(r * conv_rows, conv_rows)
            accs = [None, None]
            for j in range(W):
                e = first + j
                t = taps[j] * sh_ref[c, e % SUBLANES, pl.ds(r0 + (e - e % SUBLANES), conv_rows), :]
                accs[j % 2] = t if accs[j % 2] is None else accs[j % 2] + t
            h_ref[c, pl.ds(r0, conv_rows), :] = accs[0] + accs[1]
            return carry

        lax.fori_loop(0, tt // conv_rows, conv_rows_fn, 0)

    def norm_rows_fn(r, carry):
        r0 = pl.multiple_of(r * norm_rows, norm_rows)
        xs = [h_ref[c, pl.ds(r0, norm_rows), :] + bdw_ref[:, cs] for c, cs in enumerate(groups)]
        mu = jnp.sum(functools.reduce(jnp.add, xs), axis=-1, keepdims=True) / C
        ds = [x - mu for x in xs]
        var = jnp.sum(functools.reduce(jnp.add, [d * d for d in ds]), axis=-1, keepdims=True) / C
        inv = lax.rsqrt(var + LN_EPS)
        for d, cs in zip(ds, groups):
            y = d * inv * g_ref[:, cs] + b_ref[:, cs]
            o_ref[pl.ds(r0, norm_rows), cs] = (y * jax.nn.sigmoid(y)).astype(o_ref.dtype)
        return carry

    lax.fori_loop(0, tt // norm_rows, norm_rows_fn, 0)


def _conv_prompt(u, l, w_dw, b_dw, g, b, B, T):
    C = u.shape[1]
    W = w_dw.shape[1]
    vec = pl.BlockSpec((None, 1, C), lambda bi, i: (l, 0, 0))
    assert W - 1 <= CONV_HALO and T % CONV_HALO == 0
    tt = _tile(T, 256, CONV_HALO)
    nt = T // tt
    copy_rows = _tile(tt + CONV_HALO - SUBLANES, 64, SUBLANES)
    return pl.pallas_call(
        functools.partial(_conv_body, conv_rows=_tile(tt, 64, SUBLANES), norm_rows=_tile(tt, 128, 16),
                          copy_rows=copy_rows),
        grid=(B, nt),
        in_specs=[pl.BlockSpec((CONV_HALO, C), lambda bi, i: (jnp.maximum((bi * T + i * tt) // CONV_HALO - 1, 0), 0)),
                  pl.BlockSpec((tt, C), lambda bi, i: (bi * nt + i, 0)),
                  pl.BlockSpec((None, W, C), lambda bi, i: (l, 0, 0)), vec, vec, vec],
        out_specs=pl.BlockSpec((tt, C), lambda bi, i: (bi * nt + i, 0)),
        out_shape=jax.ShapeDtypeStruct((B * T, C), BF16),
        scratch_shapes=[pltpu.VMEM((C // LANES, SUBLANES, CONV_HALO + tt, LANES), F32),
                        pltpu.VMEM((C // LANES, tt, LANES), F32)],
        compiler_params=_cparams("parallel", "parallel"), name="conv_prompt")(u, u, w_dw, b_dw, g, b)


def _fox_body(q_ref, k_ref, v_ref, crow_ref, o_ref, *, H, Dh, scale):
    qi = pl.program_id(1)
    tq = q_ref.shape[0]
    row = lax.broadcasted_iota(jnp.int32, (tq, tq), 0)
    col = lax.broadcasted_iota(jnp.int32, (tq, tq), 1)
    causal = row >= col
    reps = tq // LANES

    def key_block(j, states, masked):
        k0 = pl.multiple_of(j * tq, tq)
        new_states = []
        for h in range(H):
            sl = slice(h * Dh, (h + 1) * Dh)
            m, l, acc = states[h]
            s = (_dot_nt(q_ref[:, sl], k_ref[pl.ds(k0, tq), sl]) * (scale * LOG2_E)
                 - crow_ref[h:h + 1, pl.ds(k0, tq)] * LOG2_E)
            if masked:
                s = jnp.where(causal, s, -jnp.inf)
            m_new = jnp.maximum(m, jnp.broadcast_to(jnp.max(s, axis=-1, keepdims=True), (tq, LANES)))
            a = jnp.exp2(m - m_new)
            p = jnp.exp2(s - jnp.concatenate([m_new] * reps, axis=1))
            l = a * l + functools.reduce(jnp.add, [p[:, c * LANES:(c + 1) * LANES] for c in range(reps)])
            acc = a * acc + _dot(p.astype(BF16), v_ref[pl.ds(k0, tq), sl])
            new_states.append((m_new, l, acc))
        return tuple(new_states)

    init = tuple((jnp.full((tq, LANES), -jnp.inf, F32), jnp.zeros((tq, LANES), F32), jnp.zeros((tq, Dh), F32))
                 for _ in range(H))
    states = lax.fori_loop(0, qi, functools.partial(key_block, masked=False), init)
    states = key_block(qi, states, True)
    for h in range(H):
        _, l, acc = states[h]
        o_ref[:, h * Dh:(h + 1) * Dh] = (acc / jnp.sum(l, axis=-1, keepdims=True)).astype(o_ref.dtype)


def _fox_prompt(q, k, v, crow, B, T, H, Dh):
    DF = H * Dh
    assert Dh == LANES
    tq = _tile(T, 512, LANES)
    nq = T // tq
    return pl.pallas_call(
        functools.partial(_fox_body, H=H, Dh=Dh, scale=Dh ** -0.5),
        grid=(B, nq),
        in_specs=[pl.BlockSpec((tq, DF), lambda b, i: (b * nq + i, 0)),
                  pl.BlockSpec((T, DF), lambda b, i: (b, 0)),
                  pl.BlockSpec((T, DF), lambda b, i: (b, 0)),
                  pl.BlockSpec((None, SUBLANES, T), lambda b, i: (b, 0, 0))],
        out_specs=pl.BlockSpec((tq, DF), lambda b, i: (b * nq + i, 0)),
        out_shape=jax.ShapeDtypeStruct((B * T, DF), BF16),
        compiler_params=_cparams("parallel", "parallel"), name="fox_prompt")(q, k, v, crow)


def _memattn_body(q_ref, k_ref, v_ref, o_ref, *, H, Dh, scale):
    for h in range(H):
        sl = slice(h * Dh, (h + 1) * Dh)
        s = _dot_nt(q_ref[:, sl], k_ref[:, sl]) * scale
        e = jnp.exp(s - jnp.max(s, axis=-1, keepdims=True))
        p = e / jnp.sum(e, axis=-1, keepdims=True)
        o_ref[:, sl] = _dot(p.astype(BF16), v_ref[:, sl]).astype(o_ref.dtype)


def _memattn_prompt(q, mk, mv, B, T, H, Dh):
    DM = H * Dh
    N = mk.shape[0] // B
    tq = _tile(T, 512, LANES)
    nq = T // tq
    return pl.pallas_call(
        functools.partial(_memattn_body, H=H, Dh=Dh, scale=Dh ** -0.5),
        grid=(B, nq),
        in_specs=[pl.BlockSpec((tq, DM), lambda b, i: (b * nq + i, 0)),
                  pl.BlockSpec((N, DM), lambda b, i: (b, 0)),
                  pl.BlockSpec((N, DM), lambda b, i: (b, 0))],
        out_specs=pl.BlockSpec((tq, DM), lambda b, i: (b * nq + i, 0)),
        out_shape=jax.ShapeDtypeStruct((B * T, DM), BF16),
        compiler_params=_cparams("parallel", "parallel"), name="memattn_prompt")(q, mk, mv)


def _mix_body(x_ref, *refs, nb):
    h_refs, w_refs, wg_refs, bg_refs = (refs[i * nb:(i + 1) * nb] for i in range(4))
    o_ref = refs[4 * nb]
    x = x_ref[...]
    mix = None
    for h_ref, w_ref, wg_ref, bg_ref in zip(h_refs, w_refs, wg_refs, bg_refs):
        term = jax.nn.sigmoid(_dot(x, wg_ref[...]) + bg_ref[...]) * _dot(h_ref[...], w_ref[...])
        mix = term if mix is None else mix + term
    o_ref[...] = mix.astype(o_ref.dtype)


def _mix(x16, branches, w_outs, w_tail, l, gate_col0, b_gate, *, tm_pref=512, tn_pref=512):
    M, D = x16.shape
    nb = len(branches)
    tm = _tile(M, tm_pref, 16)
    tn = _tile(D, tn_pref, LANES)
    assert gate_col0 % tn == 0
    in_specs = [pl.BlockSpec((tm, D), lambda i, j: (i, 0))]
    in_specs += [pl.BlockSpec((tm, h.shape[1]), lambda i, j: (i, 0)) for h in branches]
    in_specs += [pl.BlockSpec((None, w.shape[1], tn), lambda i, j: (l, 0, j)) for w in w_outs]
    in_specs += [pl.BlockSpec((None, D, tn), lambda i, j, off=(gate_col0 + b * D) // tn: (l, 0, j + off))
                 for b in range(nb)]
    in_specs += [pl.BlockSpec((None, 1, tn), lambda i, j, off=(b * D) // tn: (l, 0, j + off)) for b in range(nb)]
    return pl.pallas_call(
        functools.partial(_mix_body, nb=nb),
        grid=(M // tm, D // tn), in_specs=in_specs,
        out_specs=pl.BlockSpec((tm, tn), lambda i, j: (i, j)),
        out_shape=jax.ShapeDtypeStruct((M, D), BF16),
        compiler_params=_cparams("parallel", "parallel"), name="gated_mix",
    )(x16, *branches, *w_outs, *([w_tail] * nb), *([b_gate] * nb))


def _wo_ln_body(mix_ref, w_ref, x_ref, g_ref, b_ref, o32_ref, o16_ref, *, alpha):
    half = x_ref.shape[0] // 2
    for rows in (slice(0, half), slice(half, None)):
        y = _layer_norm(alpha * x_ref[rows, :] + _dot(mix_ref[rows, :], w_ref[...]), g_ref[...], b_ref[...])
        o32_ref[rows, :] = y
        o16_ref[rows, :] = y.astype(o16_ref.dtype)


def _wo_ln(mix, w_o, l, x32, g, b, alpha, *, tm_pref=512):
    M, D = x32.shape
    tm = _tile(M, tm_pref, 16)
    row = lambda i: (i, 0)
    layer = lambda i: (l, 0, 0)
    return pl.pallas_call(
        functools.partial(_wo_ln_body, alpha=alpha),
        grid=(M // tm,),
        in_specs=[pl.BlockSpec((tm, D), row), pl.BlockSpec((None, D, D), layer), pl.BlockSpec((tm, D), row),
                  pl.BlockSpec((None, 1, D), layer), pl.BlockSpec((None, 1, D), layer)],
        out_specs=[pl.BlockSpec((tm, D), row), pl.BlockSpec((tm, D), row)],
        out_shape=[jax.ShapeDtypeStruct((M, D), F32), jax.ShapeDtypeStruct((M, D), BF16)],
        compiler_params=_cparams("parallel"), name="wo_ln")(mix, w_o, x32, g, b)


def _mlp_body(x_ref, wu_ref, wd_ref, g_ref, b_ref, o32_ref, o16_ref, xb_ref, *, alpha):
    f = pl.program_id(1)

    @pl.when(f == 0)
    def _():
        xb_ref[...] = x_ref[...].astype(BF16)

    h = jnp.maximum(_dot(xb_ref[...], wu_ref[...]), 0.0)
    y = _dot((h * h).astype(BF16), wd_ref[...])

    @pl.when(f == 0)
    def _():
        o32_ref[...] = y

    @pl.when(f > 0)
    def _():
        o32_ref[...] += y

    @pl.when(f == pl.num_programs(1) - 1)
    def _():
        out = _layer_norm(alpha * x_ref[...] + o32_ref[...], g_ref[...], b_ref[...])
        o32_ref[...] = out
        o16_ref[...] = out.astype(o16_ref.dtype)


def _mlp(x32, w_up, w_down, l, g, b, alpha, *, tm_pref=512, tf_pref=1024):
    M, D = x32.shape
    DFF = w_up.shape[2]
    tm = _tile(M, tm_pref, 16)
    tf = _tile(DFF, tf_pref, LANES)
    row = lambda i, f: (i, 0)
    layer = lambda i, f: (l, 0, 0)
    return pl.pallas_call(
        functools.partial(_mlp_body, alpha=alpha),
        grid=(M // tm, DFF // tf),
        in_specs=[pl.BlockSpec((tm, D), row), pl.BlockSpec((None, D, tf), lambda i, f: (l, 0, f)),
                  pl.BlockSpec((None, tf, D), lambda i, f: (l, f, 0)),
                  pl.BlockSpec((None, 1, D), layer), pl.BlockSpec((None, 1, D), layer)],
        out_specs=[pl.BlockSpec((tm, D), row), pl.BlockSpec((tm, D), row)],
        out_shape=[jax.ShapeDtypeStruct((M, D), F32), jax.ShapeDtypeStruct((M, D), BF16)],
        scratch_shapes=[pltpu.VMEM((tm, D), BF16)],
        compiler_params=_cparams("parallel", "arbitrary"), name="mlp_ln")(x32, w_up, w_down, g, b)


def _conv_sample_body(s_ref, u_ref, w_ref, bdw_ref, g_ref, b_ref, h_ref, ns_ref):
    s = s_ref[...]
    u = u_ref[...]
    Wm1 = s.shape[1]
    w = w_ref[...]
    acc = jnp.sum(s * w[None, :Wm1, :], axis=1, keepdims=True) + u * w[None, Wm1:, :]
    y = _layer_norm(acc + bdw_ref[...][None], g_ref[...][None], b_ref[...][None])
    h_ref[...] = y * jax.nn.sigmoid(y)
    ns_ref[:, 0:Wm1 - 1, :] = s[:, 1:, :]
    ns_ref[:, Wm1 - 1:Wm1, :] = u


def _conv_sample(state, l, u, w_dw, b_dw, g, b):
    _, Bs, Wm1, C = state.shape
    bt = _tile(Bs, 16, 1)
    blk = lambda i: (i, 0, 0)
    vec = pl.BlockSpec((None, 1, C), lambda i: (l, 0, 0))
    return pl.pallas_call(
        _conv_sample_body,
        grid=(Bs // bt,),
        in_specs=[pl.BlockSpec((None, bt, Wm1, C), lambda i: (l, i, 0, 0)), pl.BlockSpec((bt, 1, C), blk),
                  pl.BlockSpec((None, Wm1 + 1, C), lambda i: (l, 0, 0)), vec, vec, vec],
        out_specs=[pl.BlockSpec((bt, 1, C), blk), pl.BlockSpec((bt, Wm1, C), blk)],
        out_shape=[jax.ShapeDtypeStruct((Bs, 1, C), F32), jax.ShapeDtypeStruct((Bs, Wm1, C), F32)],
        compiler_params=_cparams("parallel"), name="conv_sample")(state, u.reshape(Bs, 1, C), w_dw, b_dw, g, b)


def _scores3(k3, qs):
    return jnp.sum(k3 * qs[None], axis=-1, keepdims=True)


def _online_update(state, s3, v3):
    m, l, acc = state
    m_new = jnp.maximum(m, jnp.max(s3, axis=0))
    a = jnp.exp(m - m_new)
    e3 = jnp.exp(s3 - m_new[None])
    return m_new, a * l + jnp.sum(e3, axis=0), a * acc + jnp.sum(e3 * v3, axis=0)


def _softmax_state(H, Dh):
    return jnp.full((H, 1), -jnp.inf, F32), jnp.zeros((H, 1), F32), jnp.zeros((H, Dh), F32)


def _fox_sample_body(pt_ref, q_ref, kn_ref, vn_ref, fn_ref, *refs, n_pages, scale):
    del pt_ref
    k_refs, v_refs, f_refs = (refs[i * n_pages:(i + 1) * n_pages] for i in range(3))
    o_ref = refs[3 * n_pages]
    H, Dh = q_ref.shape
    PS = k_refs[0].shape[0]
    assert Dh == LANES
    qs = q_ref[...] * scale
    later = lax.broadcasted_iota(jnp.int32, (PS, H, PS), 2) > lax.broadcasted_iota(jnp.int32, (PS, H, PS), 0)
    ones = jnp.ones((Dh + 2 * PS, LANES), BF16)
    after = jnp.broadcast_to(fn_ref[...], (H, LANES))
    tops, sums, vals = [], [], []
    for p in range(n_pages - 1, -1, -1):
        ft = f_refs[p][...]
        hi = ft.astype(BF16).astype(F32)
        terms = [k_refs[p][...] * qs[None]] + [jnp.where(later, x[None], 0.0) for x in (hi, ft - hi)]
        lhs = jnp.concatenate([t.reshape(PS * H, t.shape[2]).astype(BF16) for t in terms], axis=1)
        raw = _dot(lhs, ones).reshape(PS, H, LANES)
        top = jnp.max(raw, axis=0)
        e3 = jnp.exp(raw - top[None])
        tops.append(top + after)
        sums.append(jnp.sum(e3, axis=0))
        vals.append(jnp.sum(e3 * v_refs[p][...], axis=0))
        after = after + jnp.sum(ft, axis=-1, keepdims=True)
    s_new = jnp.broadcast_to(jnp.sum(kn_ref[...] * qs, axis=-1, keepdims=True), (H, LANES))
    m = functools.reduce(jnp.maximum, tops + [s_new])
    e_new = jnp.exp(s_new - m)
    l = e_new
    acc = e_new * vn_ref[...]
    for top, psum, pval in zip(tops, sums, vals):
        w = jnp.exp(top - m)
        l = l + w * psum
        acc = acc + w * pval
    o_ref[...] = acc / l


def _fox_sample(q, k_new, v_new, logf_new, cache_k, cache_v, cache_logf_t, l, page_table):
    Bs, H, Dh = q.shape
    PS = cache_k.shape[2]
    n_pages = page_table.shape[1]
    one = lambda width: pl.BlockSpec((None, H, width), lambda b, pt: (b, 0, 0))
    kv_page = lambda p: pl.BlockSpec((None, None, PS, H, Dh), lambda b, pt: (l, pt[b, p], 0, 0, 0))
    f_page = lambda p: pl.BlockSpec((None, None, H, PS), lambda b, pt: (l, pt[b, p], 0, 0))
    in_specs = [one(Dh), one(Dh), one(Dh), one(1)]
    in_specs += [kv_page(p) for p in range(n_pages)] + [kv_page(p) for p in range(n_pages)]
    in_specs += [f_page(p) for p in range(n_pages)]
    return pl.pallas_call(
        functools.partial(_fox_sample_body, n_pages=n_pages, scale=Dh ** -0.5),
        grid_spec=pltpu.PrefetchScalarGridSpec(
            num_scalar_prefetch=1, grid=(Bs,), in_specs=in_specs,
            out_specs=pl.BlockSpec((None, H, Dh), lambda b, pt: (b, 0, 0))),
        out_shape=jax.ShapeDtypeStruct((Bs, H, Dh), F32),
        compiler_params=_cparams("arbitrary"), name="fox_sample",
    )(page_table, q, k_new, v_new, logf_new,
      *([cache_k] * n_pages), *([cache_v] * n_pages), *([cache_logf_t] * n_pages))


def _mem_sample_body(q_ref, k_ref, v_ref, o_ref, *, scale):
    G, H, Dh = q_ref.shape
    for g in range(G):
        qs = q_ref[g] * scale
        _, l, acc = _online_update(_softmax_state(H, Dh), _scores3(k_ref[g], qs), v_ref[g])
        o_ref[g] = acc / l


def _mem_sample(q, mem_k, mem_v, l):
    _, Bs, N, H, Dh = mem_k.shape
    G = _tile(Bs, 4, 1)
    blk = lambda i: (i, 0, 0)
    cache = pl.BlockSpec((None, G, N, H, Dh), lambda i: (l, i, 0, 0, 0))
    return pl.pallas_call(
        functools.partial(_mem_sample_body, scale=Dh ** -0.5),
        grid=(Bs // G,),
        in_specs=[pl.BlockSpec((G, H, Dh), blk), cache, cache],
        out_specs=pl.BlockSpec((G, H, Dh), blk),
        out_shape=jax.ShapeDtypeStruct((Bs, H, Dh), F32),
        compiler_params=_cparams("parallel"), name="mem_sample")(q, mem_k, mem_v)


def kernel(x_prompt, x_sample, mem_prompt, cache_k, cache_v, cache_logf, page_table, state_conv, cache_mem_k, cache_mem_v, w_in, b_f, b_gate, w_dw, b_dw, conv_ln_g, conv_ln_b, w_conv_out, w_fox_out, w_mem_kv, w_mem_out, w_o, ln1_g, ln1_b, w_up, w_down, ln2_g, ln2_b):
    depth = w_in.shape[0]
    B, T, D = x_prompt.shape
    Bs, Ts, _ = x_sample.shape
    assert Ts == 1
    W, C = w_dw.shape[1:]
    H, Dh = cache_k.shape[3:]
    DF = H * Dh
    NM, Hm, Dhm = cache_mem_k.shape[2:]
    DM = Hm * Dhm
    n_pool, PS = cache_k.shape[1:3]
    NB = b_gate.shape[1] // D
    alpha = float((2 * depth) ** 0.25)

    c_q, c_k, c_v, c_f = 2 * C, 2 * C + DF, 2 * C + 2 * DF, 2 * C + 3 * DF
    c_tail = c_f + H

    w_main = w_in.astype(BF16)
    w_f = jnp.pad(w_main[:, :, c_f:c_tail], ((0, 0), (0, 0), (0, LANES - H)))
    w_tail = w_main[:, :, c_tail:]
    vec = lambda a: a[:, None, :]
    b_f_pad = vec(jnp.pad(b_f, ((0, 0), (0, LANES - H))))
    w_co, w_fo, w_mo = w_conv_out.astype(BF16), w_fox_out.astype(BF16), w_mem_out.astype(BF16)
    w_mkv, w_o16 = w_mem_kv.astype(BF16), w_o.astype(BF16)
    w_up16, w_down16 = w_up.astype(BF16), w_down.astype(BF16)
    b_gate3, ln1_g3, ln1_b3, ln2_g3, ln2_b3 = vec(b_gate), vec(ln1_g), vec(ln1_b), vec(ln2_g), vec(ln2_b)
    conv_args = (w_dw, vec(b_dw), vec(conv_ln_g), vec(conv_ln_b))

    def project(x16, l, name, kv_carry=None):
        u, = _mm(x16, w_main, l, [0, C], C, [F32], epilogue="glu", name=name + "_glu")
        q16, = _mm(x16, w_main, l, [c_q], DF, [BF16], name=name + "_q")
        per_layer = [None, None] if kv_carry is None else [(depth, c) for c in kv_carry]
        k32, k16 = _mm(x16, w_main, l, [c_k], DF, [F32, BF16], head_dim=Dh, per_layer=per_layer[0], name=name + "_k")
        v32, v16 = _mm(x16, w_main, l, [c_v], DF, [F32, BF16], head_dim=Dh, per_layer=per_layer[1], name=name + "_v")
        logf_pad, = _mm(x16, w_f, l, [0], LANES, [F32], epilogue="logsig", bias=b_f_pad, name=name + "_logf")
        qm16, = _mm(x16, w_tail, l, [0], DM, [BF16], name=name + "_qm")
        return u, q16, k32, k16, v32, v16, logf_pad, qm16

    def merge(x32, x16, branches, l):
        mix = _mix(x16, branches, [w_co, w_fo, w_mo], w_tail, l, DM, b_gate3)
        x1_32, _ = _wo_ln(mix, w_o16, l, x32, ln1_g3, ln1_b3, alpha)
        return _mlp(x1_32, w_up16, w_down16, l, ln2_g3, ln2_b3, alpha)

    xp32 = x_prompt.reshape(B * T, D)
    xp16 = xp32.astype(BF16)
    xs32 = x_sample.reshape(Bs, D)
    xs16 = xs32.astype(BF16)
    mem16 = mem_prompt.reshape(B * NM, D).astype(BF16)
    cache_logf_t = jnp.swapaxes(cache_logf, 2, 3)
    outs = [[] for _ in range(10)]
    k_prompt = v_prompt = None
    for l in range(depth):
        u, q16, k_prompt, k16, v_prompt, v16, logf_pad, qm16 = project(xp16, l, "prompt", (k_prompt, v_prompt))
        hc = _conv_prompt(u, l, *conv_args, B, T)
        crow = _cumsum(logf_pad, B, T)
        fa = _fox_prompt(q16, k16, v16, crow, B, T, H, Dh)
        mk32, mk16 = _mm(mem16, w_mkv, l, [0], DM, [F32, BF16], name="mem_k")
        mv32, mv16 = _mm(mem16, w_mkv, l, [DM], DM, [F32, BF16], name="mem_v")
        ma = _memattn_prompt(qm16, mk16, mv16, B, T, Hm, Dhm)
        xp32, xp16 = merge(xp32, xp16, [hc, fa, ma], l)
        outs[2].append(logf_pad[:, :H].reshape(B, T, H))
        outs[3].append(u.reshape(B, T, C)[:, T - (W - 1):])
        outs[4].append(mk32.reshape(B, NM, Hm, Dhm))
        outs[5].append(mv32.reshape(B, NM, Hm, Dhm))

        u, q16, k32, k16, v32, v16, logf_pad, qm16 = project(xs16, l, "sample")
        hc3, new_state = _conv_sample(state_conv, l, u, *conv_args)
        heads = lambda a: a.astype(F32).reshape(Bs, H, Dh)
        fa = _fox_sample(heads(q16), k32, v32, logf_pad[:, :H, None], cache_k, cache_v, cache_logf_t,
                         l, page_table)
        ma = _mem_sample(qm16.astype(F32).reshape(Bs, Hm, Dhm), cache_mem_k, cache_mem_v, l)
        branches = [hc3.reshape(Bs, C), fa.reshape(Bs, DF), ma.reshape(Bs, DM)]
        xs32, xs16 = merge(xs32, xs16, [a.astype(BF16) for a in branches], l)
        outs[6].append(k32.reshape(Bs, Ts, H, Dh))
        outs[7].append(v32.reshape(Bs, Ts, H, Dh))
        outs[8].append(logf_pad[:, :H].reshape(Bs, Ts, H))
        outs[9].append(new_state)

    stacked = [None if not o else jnp.stack(o) for o in outs]
    stacked[0] = k_prompt.reshape(depth, B, T, H, Dh)
    stacked[1] = v_prompt.reshape(depth, B, T, H, Dh)
    return (xp32.reshape(B, T, D), xs32.reshape(Bs, Ts, D)) + tuple(stacked)
```

```python
import functools

import jax
import jax.numpy as jnp
from jax import lax
from jax.experimental import pallas as pl
from jax.experimental.pallas import tpu as pltpu

F32 = jnp.float32
BF16 = jnp.bfloat16

LN_EPS = 1e-5
LOG2_E = 1.4426950408889634
LANES = 128
SUBLANES = 8
VMEM_LIMIT_BYTES = 56 * 1024 * 1024
CONV_HALO = 32


def _cparams(*sem):
    return pltpu.CompilerParams(dimension_semantics=sem, vmem_limit_bytes=VMEM_LIMIT_BYTES)


def _tile(n, pref, mult):
    t = min(pref, n)
    t -= t % mult
    while t >= mult:
        if n % t == 0:
            return t
        t -= mult
    return n


def _layer_norm(r, g, b):
    mu = jnp.mean(r, axis=-1, keepdims=True)
    d = r - mu
    var = jnp.mean(d * d, axis=-1, keepdims=True)
    return d * lax.rsqrt(var + LN_EPS) * g + b


def _split3(f):
    hi = f.astype(BF16)
    r1 = f - hi.astype(F32)
    mid = r1.astype(BF16)
    lo = (r1 - mid.astype(F32)).astype(BF16)
    return hi, mid, lo


def _dot(a, b):
    return jnp.dot(a, b, preferred_element_type=F32)


def _dot_nt(a, b):
    return lax.dot_general(a, b, (((1,), (1,)), ((), ())), preferred_element_type=F32)


def _mm_body(*refs, n_w, epilogue, has_bias, has_carry):
    x_ref = refs[0]
    w_refs = refs[1:1 + n_w]
    pos = 1 + n_w
    b_ref = refs[pos] if has_bias else None
    o_refs = refs[pos + int(has_bias) + int(has_carry):]
    x = x_ref[...]
    zs = [_dot(x, w[...]) for w in w_refs]
    if epilogue == "glu":
        y = zs[0] * jax.nn.sigmoid(zs[1])
    elif epilogue == "logsig":
        y = jax.nn.log_sigmoid(zs[0] + b_ref[...])
    else:
        y = zs[0]
    for o in o_refs:
        if len(o.shape) == 3:
            Dh = o.shape[2]
            for h in range(o.shape[1]):
                o[:, h, :] = y[:, h * Dh:(h + 1) * Dh].astype(o.dtype)
        else:
            o[...] = y.astype(o.dtype)


def _mm(x, w, l, col_starts, n, out_dtypes, *, epilogue="none", bias=None, head_dim=None, per_layer=None,
        tm_pref=1024, tn_pref=1024, name="mm"):
    M, K = x.shape
    tm = _tile(M, tm_pref, 16)
    tn = n if head_dim else _tile(n, tn_pref, LANES)
    for c in col_starts:
        assert c % tn == 0
    in_specs = [pl.BlockSpec((tm, K), lambda i, j: (i, 0))]
    args = [x]
    for c in col_starts:
        in_specs.append(pl.BlockSpec((None, K, tn), lambda i, j, off=c // tn: (l, 0, j + off)))
        args.append(w)
    if bias is not None:
        in_specs.append(pl.BlockSpec((None, 1, tn), lambda i, j: (l, 0, j)))
        args.append(bias)
    out_shape = [jax.ShapeDtypeStruct((M, n), dt) for dt in out_dtypes]
    out_specs = [pl.BlockSpec((tm, tn), lambda i, j: (i, j)) for _ in out_dtypes]
    aliases = {}
    if head_dim:
        out_shape[0] = jax.ShapeDtypeStruct((M, n // head_dim, head_dim), out_dtypes[0])
        out_specs[0] = pl.BlockSpec((tm, n // head_dim, head_dim), lambda i, j: (i, 0, 0))
    if per_layer is not None:
        depth, carried = per_layer
        out_shape[0] = jax.ShapeDtypeStruct((depth, M, n // head_dim, head_dim), out_dtypes[0])
        out_specs[0] = pl.BlockSpec((None, tm, n // head_dim, head_dim), lambda i, j: (l, i, 0, 0))
        if carried is not None:
            aliases = {len(args): 0}
            in_specs.append(pl.BlockSpec(memory_space=pl.ANY))
            args.append(carried)
    outs = pl.pallas_call(
        functools.partial(_mm_body, n_w=len(col_starts), epilogue=epilogue, has_bias=bias is not None,
                          has_carry=bool(aliases)),
        grid=(M // tm, n // tn), in_specs=in_specs, out_specs=out_specs, out_shape=out_shape,
        input_output_aliases=aliases, compiler_params=_cparams("parallel", "parallel"), name=name)(*args)
    return outs


def _cumsum_body(f_ref, crow_ref, ccol_ref, *, chunk):
    T = f_ref.shape[0]
    r = lax.broadcasted_iota(jnp.int32, (chunk, chunk), 0)
    c = lax.broadcasted_iota(jnp.int32, (chunk, chunk), 1)
    tri = jnp.where(c <= r, 1.0, 0.0).astype(BF16)
    carry = jnp.zeros((1, f_ref.shape[1]), F32)
    for ci in range(T // chunk):
        hi, mid, lo = _split3(f_ref[ci * chunk:(ci + 1) * chunk, :])
        cs = _dot(tri, hi) + _dot(tri, mid) + _dot(tri, lo) + carry
        ccol_ref[ci * chunk:(ci + 1) * chunk, :] = cs
        carry = cs[chunk - 1:chunk, :]
    crow_ref[...] = ccol_ref[...].T


def _cumsum(logf_pad, B, T):
    chunk = _tile(T, 256, LANES)
    return pl.pallas_call(
        functools.partial(_cumsum_body, chunk=chunk),
        grid=(B,),
        in_specs=[pl.BlockSpec((T, LANES), lambda b: (b, 0))],
        out_specs=pl.BlockSpec((None, LANES, T), lambda b: (b, 0, 0)),
        out_shape=jax.ShapeDtypeStruct((B, LANES, T), F32),
        scratch_shapes=[pltpu.VMEM((T, LANES), F32)],
        compiler_params=_cparams("parallel"), name="logf_cumsum")(logf_pad)


def _conv_body(prev_ref, cur_ref, w_ref, bdw_ref, g_ref, b_ref, o_ref, sh_ref, h_ref, *, conv_rows, norm_rows, copy_rows):
    i = pl.program_id(1)
    tt, C = cur_ref.shape
    W = w_ref.shape[0]
    groups = [slice(c * LANES, (c + 1) * LANES) for c in range(C // LANES)]
    n_shifted = tt + CONV_HALO - SUBLANES
    first = CONV_HALO - (W - 1)

    for c, cs in enumerate(groups):
        sh_ref[c, 0, 0:CONV_HALO, :] = jnp.where(i > 0, prev_ref[:, cs], 0.0)
        sh_ref[c, 0, CONV_HALO:, :] = cur_ref[:, cs]
        for s in range(1, SUBLANES):
            for r in range(0, n_shifted, copy_rows):
                sh_ref[c, s, r:r + copy_rows, :] = sh_ref[c, 0, r + s:r + s + copy_rows, :]
        taps = [w_ref[j:j + 1, cs] for j in range(W)]

        def conv_rows_fn(r, carry):
            r0 = pl.multiple_of(r * conv_rows, conv_rows)
            accs = [None, None]
            for j in range(W):
                e = first + j
                t = taps[j] * sh_ref[c, e % SUBLANES, pl.ds(r0 + (e - e % SUBLANES), conv_rows), :]
                accs[j % 2] = t if accs[j % 2] is None else accs[j % 2] + t
            h_ref[c, pl.ds(r0, conv_rows), :] = accs[0] + accs[1]
            return carry

        lax.fori_loop(0, tt // conv_rows, conv_rows_fn, 0)

    def norm_rows_fn(r, carry):
        r0 = pl.multiple_of(r * norm_rows, norm_rows)
        xs = [h_ref[c, pl.ds(r0, norm_rows), :] + bdw_ref[:, cs] for c, cs in enumerate(groups)]
        mu = jnp.sum(functools.reduce(jnp.add, xs), axis=-1, keepdims=True) / C
        ds = [x - mu for x in xs]
        var = jnp.sum(functools.reduce(jnp.add, [d * d for d in ds]), axis=-1, keepdims=True) / C
        inv = lax.rsqrt(var + LN_EPS)
        for d, cs in zip(ds, groups):
            y = d * inv * g_ref[:, cs] + b_ref[:, cs]
            o_ref[pl.ds(r0, norm_rows), cs] = (y * jax.nn.sigmoid(y)).astype(o_ref.dtype)
        return carry

    lax.fori_loop(0, tt // norm_rows, norm_rows_fn, 0)


def _conv_prompt(u, l, w_dw, b_dw, g, b, B, T):
    C = u.shape[1]
    W = w_dw.shape[1]
    vec = pl.BlockSpec((None, 1, C), lambda bi, i: (l, 0, 0))
    assert W - 1 <= CONV_HALO and T % CONV_HALO == 0
    tt = _tile(T, 256, CONV_HALO)
    nt = T // tt
    copy_rows = _tile(tt + CONV_HALO - SUBLANES, 64, SUBLANES)
    return pl.pallas_call(
        functools.partial(_conv_body, conv_rows=_tile(tt, 64, SUBLANES), norm_rows=_tile(tt, 128, 16),
                          copy_rows=copy_rows),
        grid=(B, nt),
        in_specs=[pl.BlockSpec((CONV_HALO, C), lambda bi, i: (jnp.maximum((bi * T + i * tt) // CONV_HALO - 1, 0), 0)),
                  pl.BlockSpec((tt, C), lambda bi, i: (bi * nt + i, 0)),
                  pl.BlockSpec((None, W, C), lambda bi, i: (l, 0, 0)), vec, vec, vec],
        out_specs=pl.BlockSpec((tt, C), lambda bi, i: (bi * nt + i, 0)),
        out_shape=jax.ShapeDtypeStruct((B * T, C), BF16),
        scratch_shapes=[pltpu.VMEM((C // LANES, SUBLANES, CONV_HALO + tt, LANES), F32),
                        pltpu.VMEM((C // LANES, tt, LANES), F32)],
        compiler_params=_cparams("parallel", "parallel"), name="conv_prompt")(u, u, w_dw, b_dw, g, b)


def _fox_body(q_ref, k_ref, v_ref, crow_ref, o_ref, *, H, Dh, scale):
    qi = pl.program_id(1)
    tq = q_ref.shape[0]
    row = lax.broadcasted_iota(jnp.int32, (tq, tq), 0)
    col = lax.broadcasted_iota(jnp.int32, (tq, tq), 1)
    causal = row >= col
    reps = tq // LANES

    def key_block(j, states, masked):
        k0 = pl.multiple_of(j * tq, tq)
        new_states = []
        for h in range(H):
            sl = slice(h * Dh, (h + 1) * Dh)
            m, l, acc = states[h]
            s = (_dot_nt(q_ref[:, sl], k_ref[pl.ds(k0, tq), sl]) * (scale * LOG2_E)
                 - crow_ref[h:h + 1, pl.ds(k0, tq)] * LOG2_E)
            if masked:
                s = jnp.where(causal, s, -jnp.inf)
            m_new = jnp.maximum(m, jnp.broadcast_to(jnp.max(s, axis=-1, keepdims=True), (tq, LANES)))
            a = jnp.exp2(m - m_new)
            p = jnp.exp2(s - jnp.concatenate([m_new] * reps, axis=1))
            l = a * l + functools.reduce(jnp.add, [p[:, c * LANES:(c + 1) * LANES] for c in range(reps)])
            acc = a * acc + _dot(p.astype(BF16), v_ref[pl.ds(k0, tq), sl])
            new_states.append((m_new, l, acc))
        return tuple(new_states)

    init = tuple((jnp.full((tq, LANES), -jnp.inf, F32), jnp.zeros((tq, LANES), F32), jnp.zeros((tq, Dh), F32))
                 for _ in range(H))
    states = lax.fori_loop(0, qi, functools.partial(key_block, masked=False), init)
    states = key_block(qi, states, True)
    for h in range(H):
        _, l, acc = states[h]
        o_ref[:, h * Dh:(h + 1) * Dh] = (acc / jnp.sum(l, axis=-1, keepdims=True)).astype(o_ref.dtype)


def _fox_prompt(q, k, v, crow, B, T, H, Dh):
    DF = H * Dh
    assert Dh == LANES
    tq = _tile(T, 512, LANES)
    nq = T // tq
    return pl.pallas_call(
        functools.partial(_fox_body, H=H, Dh=Dh, scale=Dh ** -0.5),
        grid=(B, nq),
        in_specs=[pl.BlockSpec((tq, DF), lambda b, i: (b * nq + i, 0)),
                  pl.BlockSpec((T, DF), lambda b, i: (b, 0)),
                  pl.BlockSpec((T, DF), lambda b, i: (b, 0)),
                  pl.BlockSpec((None, SUBLANES, T), lambda b, i: (b, 0, 0))],
        out_specs=pl.BlockSpec((tq, DF), lambda b, i: (b * nq + i, 0)),
        out_shape=jax.ShapeDtypeStruct((B * T, DF), BF16),
        compiler_params=_cparams("parallel", "parallel"), name="fox_prompt")(q, k, v, crow)


def _memattn_body(q_ref, k_ref, v_ref, o_ref, *, H, Dh, scale):
    for h in range(H):
        sl = slice(h * Dh, (h + 1) * Dh)
        s = _dot_nt(q_ref[:, sl], k_ref[:, sl]) * scale
        e = jnp.exp(s - jnp.max(s, axis=-1, keepdims=True))
        p = e / jnp.sum(e, axis=-1, keepdims=True)
        o_ref[:, sl] = _dot(p.astype(BF16), v_ref[:, sl]).astype(o_ref.dtype)


def _memattn_prompt(q, mk, mv, B, T, H, Dh):
    DM = H * Dh
    N = mk.shape[0] // B
    tq = _tile(T, 512, LANES)
    nq = T // tq
    return pl.pallas_call(
        functools.partial(_memattn_body, H=H, Dh=Dh, scale=Dh ** -0.5),
        grid=(B, nq),
        in_specs=[pl.BlockSpec((tq, DM), lambda b, i: (b * nq + i, 0)),
                  pl.BlockSpec((N, DM), lambda b, i: (b, 0)),
                  pl.BlockSpec((N, DM), lambda b, i: (b, 0))],
        out_specs=pl.BlockSpec((tq, DM), lambda b, i: (b * nq + i, 0)),
        out_shape=jax.ShapeDtypeStruct((B * T, DM), BF16),
        compiler_params=_cparams("parallel", "parallel"), name="memattn_prompt")(q, mk, mv)


def _mix_body(x_ref, *refs, nb):
    h_refs, w_refs, wg_refs, bg_refs = (refs[i * nb:(i + 1) * nb] for i in range(4))
    o_ref = refs[4 * nb]
    x = x_ref[...]
    mix = None
    for h_ref, w_ref, wg_ref, bg_ref in zip(h_refs, w_refs, wg_refs, bg_refs):
        term = jax.nn.sigmoid(_dot(x, wg_ref[...]) + bg_ref[...]) * _dot(h_ref[...], w_ref[...])
        mix = term if mix is None else mix + term
    o_ref[...] = mix.astype(o_ref.dtype)


def _mix(x16, branches, w_outs, w_tail, l, gate_col0, b_gate, *, tm_pref=512, tn_pref=512):
    M, D = x16.shape
    nb = len(branches)
    tm = _tile(M, tm_pref, 16)
    tn = _tile(D, tn_pref, LANES)
    assert gate_col0 % tn == 0
    in_specs = [pl.BlockSpec((tm, D), lambda i, j: (i, 0))]
    in_specs += [pl.BlockSpec((tm, h.shape[1]), lambda i, j: (i, 0)) for h in branches]
    in_specs += [pl.BlockSpec((None, w.shape[1], tn), lambda i, j: (l, 0, j)) for w in w_outs]
    in_specs += [pl.BlockSpec((None, D, tn), lambda i, j, off=(gate_col0 + b * D) // tn: (l, 0, j + off))
                 for b in range(nb)]
    in_specs += [pl.BlockSpec((None, 1, tn), lambda i, j, off=(b * D) // tn: (l, 0, j + off)) for b in range(nb)]
    return pl.pallas_call(
        functools.partial(_mix_body, nb=nb),
        grid=(M // tm, D // tn), in_specs=in_specs,
        out_specs=pl.BlockSpec((tm, tn), lambda i, j: (i, j)),
        out_shape=jax.ShapeDtypeStruct((M, D), BF16),
        compiler_params=_cparams("parallel", "parallel"), name="gated_mix",
    )(x16, *branches, *w_outs, *([w_tail] * nb), *([b_gate] * nb))


def _wo_ln_body(mix_ref, w_ref, x_ref, g_ref, b_ref, o32_ref, o16_ref, *, alpha):
    half = x_ref.shape[0] // 2
    for rows in (slice(0, half), slice(half, None)):
        y = _layer_norm(alpha * x_ref[rows, :] + _dot(mix_ref[rows, :], w_ref[...]), g_ref[...], b_ref[...])
        o32_ref[rows, :] = y
        o16_ref[rows, :] = y.astype(o16_ref.dtype)


def _wo_ln(mix, w_o, l, x32, g, b, alpha, *, tm_pref=512):
    M, D = x32.shape
    tm = _tile(M, tm_pref, 16)
    row = lambda i: (i, 0)
    layer = lambda i: (l, 0, 0)
    return pl.pallas_call(
        functools.partial(_wo_ln_body, alpha=alpha),
        grid=(M // tm,),
        in_specs=[pl.BlockSpec((tm, D), row), pl.BlockSpec((None, D, D), layer), pl.BlockSpec((tm, D), row),
                  pl.BlockSpec((None, 1, D), layer), pl.BlockSpec((None, 1, D), layer)],
        out_specs=[pl.BlockSpec((tm, D), row), pl.BlockSpec((tm, D), row)],
        out_shape=[jax.ShapeDtypeStruct((M, D), F32), jax.ShapeDtypeStruct((M, D), BF16)],
        compiler_params=_cparams("parallel"), name="wo_ln")(mix, w_o, x32, g, b)


def _mlp_body(x_ref, wu_ref, wd_ref, g_ref, b_ref, o32_ref, o16_ref, xb_ref, *, alpha):
    f = pl.program_id(1)

    @pl.when(f == 0)
    def _():
        xb_ref[...] = x_ref[...].astype(BF16)
        o32_ref[...] = jnp.zeros_like(o32_ref)

    half = x_ref.shape[0] // 2
    for rows in (slice(0, half), slice(half, None)):
        h = jnp.maximum(_dot(xb_ref[rows, :], wu_ref[...]), 0.0)
        o32_ref[rows, :] += _dot((h * h).astype(BF16), wd_ref[...])

    @pl.when(f == pl.num_programs(1) - 1)
    def _():
        out = _layer_norm(alpha * x_ref[...] + o32_ref[...], g_ref[...], b_ref[...])
        o32_ref[...] = out
        o16_ref[...] = out.astype(o16_ref.dtype)


def _mlp(x32, w_up, w_down, l, g, b, alpha, *, tm_pref=512, tf_pref=1024):
    M, D = x32.shape
    DFF = w_up.shape[2]
    tm = _tile(M, tm_pref, 16)
    tf = _tile(DFF, tf_pref, LANES)
    row = lambda i, f: (i, 0)
    layer = lambda i, f: (l, 0, 0)
    return pl.pallas_call(
        functools.partial(_mlp_body, alpha=alpha),
        grid=(M // tm, DFF // tf),
        in_specs=[pl.BlockSpec((tm, D), row), pl.BlockSpec((None, D, tf), lambda i, f: (l, 0, f)),
                  pl.BlockSpec((None, tf, D), lambda i, f: (l, f, 0)),
                  pl.BlockSpec((None, 1, D), layer), pl.BlockSpec((None, 1, D), layer)],
        out_specs=[pl.BlockSpec((tm, D), row), pl.BlockSpec((tm, D), row)],
        out_shape=[jax.ShapeDtypeStruct((M, D), F32), jax.ShapeDtypeStruct((M, D), BF16)],
        scratch_shapes=[pltpu.VMEM((tm, D), BF16)],
        compiler_params=_cparams("parallel", "arbitrary"), name="mlp_ln")(x32, w_up, w_down, g, b)


def _conv_sample_body(s_ref, u_ref, w_ref, bdw_ref, g_ref, b_ref, *refs):
    h_ref, ns_ref = refs[-2:]
    Wm1 = s_ref.shape[0]
    u = u_ref[...]
    acc = u * w_ref[Wm1:Wm1 + 1, :]
    for j in range(Wm1):
        acc = acc + s_ref[j] * w_ref[j:j + 1, :]
    y = _layer_norm(acc + bdw_ref[...], g_ref[...], b_ref[...])
    h_ref[...] = (y * jax.nn.sigmoid(y)).astype(h_ref.dtype)
    ns_ref[0:Wm1 - 1] = s_ref[1:Wm1]
    ns_ref[Wm1 - 1] = u


def _conv_sample(state_t, l, u, w_dw, b_dw, g, b, carried):
    depth, Wm1, Bs, C = state_t.shape
    bt = _tile(Bs, 16, SUBLANES)
    vec = pl.BlockSpec((None, 1, C), lambda i: (l, 0, 0))
    state_spec = pl.BlockSpec((None, Wm1, bt, C), lambda i: (l, 0, i, 0))
    in_specs = [state_spec, pl.BlockSpec((bt, C), lambda i: (i, 0)),
                pl.BlockSpec((None, Wm1 + 1, C), lambda i: (l, 0, 0)), vec, vec, vec]
    args = [state_t, u, w_dw, b_dw, g, b]
    aliases = {}
    if carried is not None:
        aliases = {len(args): 1}
        in_specs.append(pl.BlockSpec(memory_space=pl.ANY))
        args.append(carried)
    return pl.pallas_call(
        _conv_sample_body,
        grid=(Bs // bt,), in_specs=in_specs,
        out_specs=[pl.BlockSpec((bt, C), lambda i: (i, 0)), state_spec],
        out_shape=[jax.ShapeDtypeStruct((Bs, C), BF16), jax.ShapeDtypeStruct((depth, Wm1, Bs, C), F32)],
        input_output_aliases=aliases, compiler_params=_cparams("parallel"), name="conv_sample")(*args)


def _scores3(k3, qs):
    return jnp.sum(k3 * qs[None], axis=-1, keepdims=True)


def _online_update(state, s3, v3):
    m, l, acc = state
    m_new = jnp.maximum(m, jnp.max(s3, axis=0))
    a = jnp.exp(m - m_new)
    e3 = jnp.exp(s3 - m_new[None])
    return m_new, a * l + jnp.sum(e3, axis=0), a * acc + jnp.sum(e3 * v3, axis=0)


def _softmax_state(H, Dh):
    return jnp.full((H, 1), -jnp.inf, F32), jnp.zeros((H, 1), F32), jnp.zeros((H, Dh), F32)


def _fox_sample_body(pt_ref, q_ref, kn_ref, vn_ref, fn_ref, *refs, n_pages, scale):
    del pt_ref
    k_refs, v_refs, f_refs = (refs[i * n_pages:(i + 1) * n_pages] for i in range(3))
    o_ref = refs[3 * n_pages]
    H, Dh = q_ref.shape
    PS = k_refs[0].shape[0]
    assert Dh == LANES
    qs = q_ref[...] * scale
    later = lax.broadcasted_iota(jnp.int32, (PS, H, PS), 2) > lax.broadcasted_iota(jnp.int32, (PS, H, PS), 0)
    ones = jnp.ones((Dh + 2 * PS, LANES), BF16)
    after = jnp.broadcast_to(fn_ref[...], (H, LANES))
    tops, sums, vals = [], [], []
    for p in range(n_pages - 1, -1, -1):
        ft = f_refs[p][...]
        hi = ft.astype(BF16).astype(F32)
        terms = [k_refs[p][...] * qs[None]] + [jnp.where(later, x[None], 0.0) for x in (hi, ft - hi)]
        lhs = jnp.concatenate([t.reshape(PS * H, t.shape[2]).astype(BF16) for t in terms], axis=1)
        raw = _dot(lhs, ones).reshape(PS, H, LANES)
        top = jnp.max(raw, axis=0)
        e3 = jnp.exp(raw - top[None])
        tops.append(top + after)
        sums.append(jnp.sum(e3, axis=0))
        vals.append(jnp.sum(e3 * v_refs[p][...], axis=0))
        after = after + jnp.sum(ft, axis=-1, keepdims=True)
    s_new = jnp.broadcast_to(jnp.sum(kn_ref[...] * qs, axis=-1, keepdims=True), (H, LANES))
    m = functools.reduce(jnp.maximum, tops + [s_new])
    e_new = jnp.exp(s_new - m)
    l = e_new
    acc = e_new * vn_ref[...]
    for top, psum, pval in zip(tops, sums, vals):
        w = jnp.exp(top - m)
        l = l + w * psum
        acc = acc + w * pval
    o_ref[...] = acc / l


def _fox_sample(q, k_new, v_new, logf_new, cache_k, cache_v, cache_logf_t, l, page_table):
    Bs, H, Dh = q.shape
    PS = cache_k.shape[2]
    n_pages = page_table.shape[1]
    one = lambda width: pl.BlockSpec((None, H, width), lambda b, pt: (b, 0, 0))
    kv_page = lambda p: pl.BlockSpec((None, None, PS, H, Dh), lambda b, pt: (l, pt[b, p], 0, 0, 0))
    f_page = lambda p: pl.BlockSpec((None, None, H, PS), lambda b, pt: (l, pt[b, p], 0, 0))
    in_specs = [one(Dh), one(Dh), one(Dh), one(1)]
    in_specs += [kv_page(p) for p in range(n_pages)] + [kv_page(p) for p in range(n_pages)]
    in_specs += [f_page(p) for p in range(n_pages)]
    return pl.pallas_call(
        functools.partial(_fox_sample_body, n_pages=n_pages, scale=Dh ** -0.5),
        grid_spec=pltpu.PrefetchScalarGridSpec(
            num_scalar_prefetch=1, grid=(Bs,), in_specs=in_specs,
            out_specs=pl.BlockSpec((None, H, Dh), lambda b, pt: (b, 0, 0))),
        out_shape=jax.ShapeDtypeStruct((Bs, H, Dh), F32),
        compiler_params=_cparams("arbitrary"), name="fox_sample",
    )(page_table, q, k_new, v_new, logf_new,
      *([cache_k] * n_pages), *([cache_v] * n_pages), *([cache_logf_t] * n_pages))


def _mem_sample_body(q_ref, k_ref, v_ref, o_ref, *, scale):
    G, H, Dh = q_ref.shape
    for g in range(G):
        qs = q_ref[g] * scale
        _, l, acc = _online_update(_softmax_state(H, Dh), _scores3(k_ref[g], qs), v_ref[g])
        o_ref[g] = acc / l


def _mem_sample(q, mem_k, mem_v, l):
    _, Bs, N, H, Dh = mem_k.shape
    G = _tile(Bs, 4, 1)
    blk = lambda i: (i, 0, 0)
    cache = pl.BlockSpec((None, G, N, H, Dh), lambda i: (l, i, 0, 0, 0))
    return pl.pallas_call(
        functools.partial(_mem_sample_body, scale=Dh ** -0.5),
        grid=(Bs // G,),
        in_specs=[pl.BlockSpec((G, H, Dh), blk), cache, cache],
        out_specs=pl.BlockSpec((G, H, Dh), blk),
        out_shape=jax.ShapeDtypeStruct((Bs, H, Dh), F32),
        compiler_params=_cparams("parallel"), name="mem_sample")(q, mem_k, mem_v)


def kernel(x_prompt, x_sample, mem_prompt, cache_k, cache_v, cache_logf, page_table, state_conv, cache_mem_k, cache_mem_v, w_in, b_f, b_gate, w_dw, b_dw, conv_ln_g, conv_ln_b, w_conv_out, w_fox_out, w_mem_kv, w_mem_out, w_o, ln1_g, ln1_b, w_up, w_down, ln2_g, ln2_b):
    depth = w_in.shape[0]
    B, T, D = x_prompt.shape
    Bs, Ts, _ = x_sample.shape
    assert Ts == 1
    W, C = w_dw.shape[1:]
    H, Dh = cache_k.shape[3:]
    DF = H * Dh
    NM, Hm, Dhm = cache_mem_k.shape[2:]
    DM = Hm * Dhm
    n_pool, PS = cache_k.shape[1:3]
    NB = b_gate.shape[1] // D
    alpha = float((2 * depth) ** 0.25)

    c_q, c_k, c_v, c_f = 2 * C, 2 * C + DF, 2 * C + 2 * DF, 2 * C + 3 * DF
    c_tail = c_f + H

    w_main = w_in.astype(BF16)
    w_f = jnp.pad(w_main[:, :, c_f:c_tail], ((0, 0), (0, 0), (0, LANES - H)))
    w_tail = w_main[:, :, c_tail:]
    vec = lambda a: a[:, None, :]
    b_f_pad = vec(jnp.pad(b_f, ((0, 0), (0, LANES - H))))
    w_co, w_fo, w_mo = w_conv_out.astype(BF16), w_fox_out.astype(BF16), w_mem_out.astype(BF16)
    w_mkv, w_o16 = w_mem_kv.astype(BF16), w_o.astype(BF16)
    w_up16, w_down16 = w_up.astype(BF16), w_down.astype(BF16)
    b_gate3, ln1_g3, ln1_b3, ln2_g3, ln2_b3 = vec(b_gate), vec(ln1_g), vec(ln1_b), vec(ln2_g), vec(ln2_b)
    conv_args = (w_dw, vec(b_dw), vec(conv_ln_g), vec(conv_ln_b))

    def project(x16, l, name, kv_carry=None):
        u, = _mm(x16, w_main, l, [0, C], C, [F32], epilogue="glu", name=name + "_glu")
        q16, = _mm(x16, w_main, l, [c_q], DF, [BF16], name=name + "_q")
        per_layer = [None, None] if kv_carry is None else [(depth, c) for c in kv_carry]
        k32, k16 = _mm(x16, w_main, l, [c_k], DF, [F32, BF16], head_dim=Dh, per_layer=per_layer[0], name=name + "_k")
        v32, v16 = _mm(x16, w_main, l, [c_v], DF, [F32, BF16], head_dim=Dh, per_layer=per_layer[1], name=name + "_v")
        logf_pad, = _mm(x16, w_f, l, [0], LANES, [F32], epilogue="logsig", bias=b_f_pad, name=name + "_logf")
        qm16, = _mm(x16, w_tail, l, [0], DM, [BF16], name=name + "_qm")
        return u, q16, k32, k16, v32, v16, logf_pad, qm16

    def merge(x32, x16, branches, l):
        mix = _mix(x16, branches, [w_co, w_fo, w_mo], w_tail, l, DM, b_gate3)
        x1_32, _ = _wo_ln(mix, w_o16, l, x32, ln1_g3, ln1_b3, alpha)
        return _mlp(x1_32, w_up16, w_down16, l, ln2_g3, ln2_b3, alpha)

    xp32 = x_prompt.reshape(B * T, D)
    xp16 = xp32.astype(BF16)
    xs32 = x_sample.reshape(Bs, D)
    xs16 = xs32.astype(BF16)
    mem16 = mem_prompt.reshape(B * NM, D).astype(BF16)
    cache_logf_t = jnp.swapaxes(cache_logf, 2, 3)
    outs = [[] for _ in range(10)]
    state_t = jnp.swapaxes(state_conv, 1, 2)
    k_prompt = v_prompt = None
    conv_state = None
    for l in range(depth):
        u, q16, k_prompt, k16, v_prompt, v16, logf_pad, qm16 = project(xp16, l, "prompt", (k_prompt, v_prompt))
        hc = _conv_prompt(u, l, *conv_args, B, T)
        crow = _cumsum(logf_pad, B, T)
        fa = _fox_prompt(q16, k16, v16, crow, B, T, H, Dh)
        mk32, mk16 = _mm(mem16, w_mkv, l, [0], DM, [F32, BF16], name="mem_k")
        mv32, mv16 = _mm(mem16, w_mkv, l, [DM], DM, [F32, BF16], name="mem_v")
        ma = _memattn_prompt(qm16, mk16, mv16, B, T, Hm, Dhm)
        xp32, xp16 = merge(xp32, xp16, [hc, fa, ma], l)
        outs[2].append(logf_pad[:, :H].reshape(B, T, H))
        outs[3].append(u.reshape(B, T, C)[:, T - (W - 1):])
        outs[4].append(mk32.reshape(B, NM, Hm, Dhm))
        outs[5].append(mv32.reshape(B, NM, Hm, Dhm))

        u, q16, k32, k16, v32, v16, logf_pad, qm16 = project(xs16, l, "sample")
        hc, conv_state = _conv_sample(state_t, l, u, *conv_args, conv_state)
        heads = lambda a: a.astype(F32).reshape(Bs, H, Dh)
        fa = _fox_sample(heads(q16), k32, v32, logf_pad[:, :H, None], cache_k, cache_v, cache_logf_t,
                         l, page_table)
        ma = _mem_sample(qm16.astype(F32).reshape(Bs, Hm, Dhm), cache_mem_k, cache_mem_v, l)
        branches = [hc, fa.reshape(Bs, DF), ma.reshape(Bs, DM)]
        xs32, xs16 = merge(xs32, xs16, [a.astype(BF16) for a in branches], l)
        outs[6].append(k32.reshape(Bs, Ts, H, Dh))
        outs[7].append(v32.reshape(Bs, Ts, H, Dh))
        outs[8].append(logf_pad[:, :H].reshape(Bs, Ts, H))

    stacked = [None if not o else jnp.stack(o) for o in outs]
    stacked[0] = k_prompt.reshape(depth, B, T, H, Dh)
    stacked[1] = v_prompt.reshape(depth, B, T, H, Dh)
    stacked[9] = jnp.swapaxes(conv_state, 1, 2)
    return (xp32.reshape(B, T, D), xs32.reshape(Bs, Ts, D)) + tuple(stacked)
```

```python
import functools

import jax
import jax.numpy as jnp
from jax import lax
from jax.experimental import pallas as pl
from jax.experimental.pallas import tpu as pltpu

F32 = jnp.float32
BF16 = jnp.bfloat16

LN_EPS = 1e-5
LOG2_E = 1.4426950408889634
LANES = 128
SUBLANES = 8
VMEM_LIMIT_BYTES = 56 * 1024 * 1024
CONV_HALO = 32


def _cparams(*sem):
    return pltpu.CompilerParams(dimension_semantics=sem, vmem_limit_bytes=VMEM_LIMIT_BYTES)


def _tile(n, pref, mult):
    t = min(pref, n)
    t -= t % mult
    while t >= mult:
        if n % t == 0:
            return t
        t -= mult
    return n


def _layer_norm(r, g, b):
    mu = jnp.mean(r, axis=-1, keepdims=True)
    d = r - mu
    var = jnp.mean(d * d, axis=-1, keepdims=True)
    return d * lax.rsqrt(var + LN_EPS) * g + b


def _split3(f):
    hi = f.astype(BF16)
    r1 = f - hi.astype(F32)
    mid = r1.astype(BF16)
    lo = (r1 - mid.astype(F32)).astype(BF16)
    return hi, mid, lo


def _dot(a, b):
    return jnp.dot(a, b, preferred_element_type=F32)


def _dot_nt(a, b):
    return lax.dot_general(a, b, (((1,), (1,)), ((), ())), preferred_element_type=F32)


def _mm_body(*refs, n_w, epilogue, has_bias, has_carry):
    x_ref = refs[0]
    w_refs = refs[1:1 + n_w]
    pos = 1 + n_w
    b_ref = refs[pos] if has_bias else None
    o_refs = refs[pos + int(has_bias) + int(has_carry):]
    x = x_ref[...]
    zs = [_dot(x, w[...]) for w in w_refs]
    if epilogue == "glu":
        y = zs[0] * jax.nn.sigmoid(zs[1])
    elif epilogue == "logsig":
        y = jax.nn.log_sigmoid(zs[0] + b_ref[...])
    else:
        y = zs[0]
    for o in o_refs:
        if len(o.shape) == 3:
            Dh = o.shape[2]
            for h in range(o.shape[1]):
                o[:, h, :] = y[:, h * Dh:(h + 1) * Dh].astype(o.dtype)
        else:
            o[...] = y.astype(o.dtype)


def _mm(x, w, l, col_starts, n, out_dtypes, *, epilogue="none", bias=None, head_dim=None, per_layer=None,
        tm_pref=1024, tn_pref=1024, name="mm"):
    M, K = x.shape
    tm = _tile(M, tm_pref, 16)
    tn = n if head_dim else _tile(n, tn_pref, LANES)
    for c in col_starts:
        assert c % tn == 0
    in_specs = [pl.BlockSpec((tm, K), lambda i, j: (i, 0))]
    args = [x]
    for c in col_starts:
        in_specs.append(pl.BlockSpec((None, K, tn), lambda i, j, off=c // tn: (l, 0, j + off)))
        args.append(w)
    if bias is not None:
        in_specs.append(pl.BlockSpec((None, 1, tn), lambda i, j: (l, 0, j)))
        args.append(bias)
    out_shape = [jax.ShapeDtypeStruct((M, n), dt) for dt in out_dtypes]
    out_specs = [pl.BlockSpec((tm, tn), lambda i, j: (i, j)) for _ in out_dtypes]
    aliases = {}
    if head_dim:
        out_shape[0] = jax.ShapeDtypeStruct((M, n // head_dim, head_dim), out_dtypes[0])
        out_specs[0] = pl.BlockSpec((tm, n // head_dim, head_dim), lambda i, j: (i, 0, 0))
    if per_layer is not None:
        depth, carried = per_layer
        out_shape[0] = jax.ShapeDtypeStruct((depth, M, n // head_dim, head_dim), out_dtypes[0])
        out_specs[0] = pl.BlockSpec((None, tm, n // head_dim, head_dim), lambda i, j: (l, i, 0, 0))
        if carried is not None:
            aliases = {len(args): 0}
            in_specs.append(pl.BlockSpec(memory_space=pl.ANY))
            args.append(carried)
    outs = pl.pallas_call(
        functools.partial(_mm_body, n_w=len(col_starts), epilogue=epilogue, has_bias=bias is not None,
                          has_carry=bool(aliases)),
        grid=(M // tm, n // tn), in_specs=in_specs, out_specs=out_specs, out_shape=out_shape,
        input_output_aliases=aliases, compiler_params=_cparams("parallel", "parallel"), name=name)(*args)
    return outs


def _cumsum_body(f_ref, crow_ref, ccol_ref, *, chunk):
    T = f_ref.shape[0]
    r = lax.broadcasted_iota(jnp.int32, (chunk, chunk), 0)
    c = lax.broadcasted_iota(jnp.int32, (chunk, chunk), 1)
    tri = jnp.where(c <= r, 1.0, 0.0).astype(BF16)
    carry = jnp.zeros((1, f_ref.shape[1]), F32)
    for ci in range(T // chunk):
        hi, mid, lo = _split3(f_ref[ci * chunk:(ci + 1) * chunk, :])
        cs = _dot(tri, hi) + _dot(tri, mid) + _dot(tri, lo) + carry
        ccol_ref[ci * chunk:(ci + 1) * chunk, :] = cs
        carry = cs[chunk - 1:chunk, :]
    crow_ref[...] = ccol_ref[...].T


def _cumsum(logf_pad, B, T):
    chunk = _tile(T, 256, LANES)
    return pl.pallas_call(
        functools.partial(_cumsum_body, chunk=chunk),
        grid=(B,),
        in_specs=[pl.BlockSpec((T, LANES), lambda b: (b, 0))],
        out_specs=pl.BlockSpec((None, LANES, T), lambda b: (b, 0, 0)),
        out_shape=jax.ShapeDtypeStruct((B, LANES, T), F32),
        scratch_shapes=[pltpu.VMEM((T, LANES), F32)],
        compiler_params=_cparams("parallel"), name="logf_cumsum")(logf_pad)


def _conv_body(prev_ref, cur_ref, w_ref, bdw_ref, g_ref, b_ref, o_ref, sh_ref, h_ref, *, conv_rows, norm_rows, copy_rows):
    i = pl.program_id(1)
    tt, C = cur_ref.shape
    W = w_ref.shape[0]
    groups = [slice(c * LANES, (c + 1) * LANES) for c in range(C // LANES)]
    n_shifted = tt + CONV_HALO - SUBLANES
    first = CONV_HALO - (W - 1)

    for c, cs in enumerate(groups):
        sh_ref[c, 0, 0:CONV_HALO, :] = jnp.where(i > 0, prev_ref[:, cs], 0.0)
        sh_ref[c, 0, CONV_HALO:, :] = cur_ref[:, cs]
        for s in range(1, SUBLANES):
            for r in range(0, n_shifted, copy_rows):
                sh_ref[c, s, r:r + copy_rows, :] = sh_ref[c, 0, r + s:r + s + copy_rows, :]
        taps = [w_ref[j:j + 1, cs] for j in range(W)]

        def conv_rows_fn(r, carry):
            r0 = pl.multiple_of(r * conv_rows, conv_rows)
            accs = [None, None]
            for j in range(W):
                e = first + j
                t = taps[j] * sh_ref[c, e % SUBLANES, pl.ds(r0 + (e - e % SUBLANES), conv_rows), :]
                accs[j % 2] = t if accs[j % 2] is None else accs[j % 2] + t
            h_ref[c, pl.ds(r0, conv_rows), :] = accs[0] + accs[1]
            return carry

        lax.fori_loop(0, tt // conv_rows, conv_rows_fn, 0)

    def norm_rows_fn(r, carry):
        r0 = pl.multiple_of(r * norm_rows, norm_rows)
        xs = [h_ref[c, pl.ds(r0, norm_rows), :] + bdw_ref[:, cs] for c, cs in enumerate(groups)]
        mu = jnp.sum(functools.reduce(jnp.add, xs), axis=-1, keepdims=True) / C
        ds = [x - mu for x in xs]
        var = jnp.sum(functools.reduce(jnp.add, [d * d for d in ds]), axis=-1, keepdims=True) / C
        inv = lax.rsqrt(var + LN_EPS)
        for d, cs in zip(ds, groups):
            y = d * inv * g_ref[:, cs] + b_ref[:, cs]
            o_ref[pl.ds(r0, norm_rows), cs] = (y * jax.nn.sigmoid(y)).astype(o_ref.dtype)
        return carry

    lax.fori_loop(0, tt // norm_rows, norm_rows_fn, 0)


def _conv_prompt(u, l, w_dw, b_dw, g, b, B, T):
    C = u.shape[1]
    W = w_dw.shape[1]
    vec = pl.BlockSpec((None, 1, C), lambda bi, i: (l, 0, 0))
    assert W - 1 <= CONV_HALO and T % CONV_HALO == 0
    tt = _tile(T, 256, CONV_HALO)
    nt = T // tt
    copy_rows = _tile(tt + CONV_HALO - SUBLANES, 64, SUBLANES)
    return pl.pallas_call(
        functools.partial(_conv_body, conv_rows=_tile(tt, 64, SUBLANES), norm_rows=_tile(tt, 128, 16),
                          copy_rows=copy_rows),
        grid=(B, nt),
        in_specs=[pl.BlockSpec((CONV_HALO, C), lambda bi, i: (jnp.maximum((bi * T + i * tt) // CONV_HALO - 1, 0), 0)),
                  pl.BlockSpec((tt, C), lambda bi, i: (bi * nt + i, 0)),
                  pl.BlockSpec((None, W, C), lambda bi, i: (l, 0, 0)), vec, vec, vec],
        out_specs=pl.BlockSpec((tt, C), lambda bi, i: (bi * nt + i, 0)),
        out_shape=jax.ShapeDtypeStruct((B * T, C), BF16),
        scratch_shapes=[pltpu.VMEM((C // LANES, SUBLANES, CONV_HALO + tt, LANES), F32),
                        pltpu.VMEM((C // LANES, tt, LANES), F32)],
        compiler_params=_cparams("parallel", "parallel"), name="conv_prompt")(u, u, w_dw, b_dw, g, b)


def _fox_body(q_ref, k_ref, v_ref, crow_ref, o_ref, *, H, Dh, scale):
    qi = pl.program_id(1)
    tq = q_ref.shape[0]
    row = lax.broadcasted_iota(jnp.int32, (tq, tq), 0)
    col = lax.broadcasted_iota(jnp.int32, (tq, tq), 1)
    causal = row >= col
    reps = tq // LANES

    def key_block(j, states, masked):
        k0 = pl.multiple_of(j * tq, tq)
        new_states = []
        for h in range(H):
            sl = slice(h * Dh, (h + 1) * Dh)
            m, l, acc = states[h]
            s = (_dot_nt(q_ref[:, sl], k_ref[pl.ds(k0, tq), sl]) * (scale * LOG2_E)
                 - crow_ref[h:h + 1, pl.ds(k0, tq)] * LOG2_E)
            if masked:
                s = jnp.where(causal, s, -jnp.inf)
            m_new = jnp.maximum(m, jnp.broadcast_to(jnp.max(s, axis=-1, keepdims=True), (tq, LANES)))
            a = jnp.exp2(m - m_new)
            p = jnp.exp2(s - jnp.concatenate([m_new] * reps, axis=1))
            l = a * l + functools.reduce(jnp.add, [p[:, c * LANES:(c + 1) * LANES] for c in range(reps)])
            acc = a * acc + _dot(p.astype(BF16), v_ref[pl.ds(k0, tq), sl])
            new_states.append((m_new, l, acc))
        return tuple(new_states)

    init = tuple((jnp.full((tq, LANES), -jnp.inf, F32), jnp.zeros((tq, LANES), F32), jnp.zeros((tq, Dh), F32))
                 for _ in range(H))
    states = lax.fori_loop(0, qi, functools.partial(key_block, masked=False), init)
    states = key_block(qi, states, True)
    for h in range(H):
        _, l, acc = states[h]
        o_ref[:, h * Dh:(h + 1) * Dh] = (acc / jnp.sum(l, axis=-1, keepdims=True)).astype(o_ref.dtype)


def _fox_prompt(q, k, v, crow, B, T, H, Dh):
    DF = H * Dh
    assert Dh == LANES
    tq = _tile(T, 512, LANES)
    nq = T // tq
    return pl.pallas_call(
        functools.partial(_fox_body, H=H, Dh=Dh, scale=Dh ** -0.5),
        grid=(B, nq),
        in_specs=[pl.BlockSpec((tq, DF), lambda b, i: (b * nq + i, 0)),
                  pl.BlockSpec((T, DF), lambda b, i: (b, 0)),
                  pl.BlockSpec((T, DF), lambda b, i: (b, 0)),
                  pl.BlockSpec((None, SUBLANES, T), lambda b, i: (b, 0, 0))],
        out_specs=pl.BlockSpec((tq, DF), lambda b, i: (b * nq + i, 0)),
        out_shape=jax.ShapeDtypeStruct((B * T, DF), BF16),
        compiler_params=_cparams("parallel", "parallel"), name="fox_prompt")(q, k, v, crow)


def _memattn_body(q_ref, k_ref, v_ref, o_ref, *, H, Dh, scale):
    for h in range(H):
        sl = slice(h * Dh, (h + 1) * Dh)
        s = _dot_nt(q_ref[:, sl], k_ref[:, sl]) * scale
        e = jnp.exp(s - jnp.max(s, axis=-1, keepdims=True))
        p = e / jnp.sum(e, axis=-1, keepdims=True)
        o_ref[:, sl] = _dot(p.astype(BF16), v_ref[:, sl]).astype(o_ref.dtype)


def _memattn_prompt(q, mk, mv, B, T, H, Dh):
    DM = H * Dh
    N = mk.shape[0] // B
    tq = _tile(T, 512, LANES)
    nq = T // tq
    return pl.pallas_call(
        functools.partial(_memattn_body, H=H, Dh=Dh, scale=Dh ** -0.5),
        grid=(B, nq),
        in_specs=[pl.BlockSpec((tq, DM), lambda b, i: (b * nq + i, 0)),
                  pl.BlockSpec((N, DM), lambda b, i: (b, 0)),
                  pl.BlockSpec((N, DM), lambda b, i: (b, 0))],
        out_specs=pl.BlockSpec((tq, DM), lambda b, i: (b * nq + i, 0)),
        out_shape=jax.ShapeDtypeStruct((B * T, DM), BF16),
        compiler_params=_cparams("parallel", "parallel"), name="memattn_prompt")(q, mk, mv)


def _mix_body(x_ref, *refs, nb):
    h_refs, w_refs, wg_refs, bg_refs = (refs[i * nb:(i + 1) * nb] for i in range(4))
    o_ref = refs[4 * nb]
    x = x_ref[...]
    mix = None
    for h_ref, w_ref, wg_ref, bg_ref in zip(h_refs, w_refs, wg_refs, bg_refs):
        term = jax.nn.sigmoid(_dot(x, wg_ref[...]) + bg_ref[...]) * _dot(h_ref[...], w_ref[...])
        mix = term if mix is None else mix + term
    o_ref[...] = mix.astype(o_ref.dtype)


def _mix(x16, branches, w_outs, w_tail, l, gate_col0, b_gate, *, tm_pref=1024, tn_pref=512):
    M, D = x16.shape
    nb = len(branches)
    tm = _tile(M, tm_pref, 16)
    tn = _tile(D, tn_pref, LANES)
    assert gate_col0 % tn == 0
    in_specs = [pl.BlockSpec((tm, D), lambda i, j: (i, 0))]
    in_specs += [pl.BlockSpec((tm, h.shape[1]), lambda i, j: (i, 0)) for h in branches]
    in_specs += [pl.BlockSpec((None, w.shape[1], tn), lambda i, j: (l, 0, j)) for w in w_outs]
    in_specs += [pl.BlockSpec((None, D, tn), lambda i, j, off=(gate_col0 + b * D) // tn: (l, 0, j + off))
                 for b in range(nb)]
    in_specs += [pl.BlockSpec((None, 1, tn), lambda i, j, off=(b * D) // tn: (l, 0, j + off)) for b in range(nb)]
    return pl.pallas_call(
        functools.partial(_mix_body, nb=nb),
        grid=(M // tm, D // tn), in_specs=in_specs,
        out_specs=pl.BlockSpec((tm, tn), lambda i, j: (i, j)),
        out_shape=jax.ShapeDtypeStruct((M, D), BF16),
        compiler_params=_cparams("parallel", "parallel"), name="gated_mix",
    )(x16, *branches, *w_outs, *([w_tail] * nb), *([b_gate] * nb))


def _wo_ln_body(mix_ref, w_ref, x_ref, g_ref, b_ref, o32_ref, o16_ref, *, alpha):
    half = x_ref.shape[0] // 2
    for rows in (slice(0, half), slice(half, None)):
        y = _layer_norm(alpha * x_ref[rows, :] + _dot(mix_ref[rows, :], w_ref[...]), g_ref[...], b_ref[...])
        o32_ref[rows, :] = y
        o16_ref[rows, :] = y.astype(o16_ref.dtype)


def _wo_ln(mix, w_o, l, x32, g, b, alpha, *, tm_pref=512):
    M, D = x32.shape
    tm = _tile(M, tm_pref, 16)
    row = lambda i: (i, 0)
    layer = lambda i: (l, 0, 0)
    return pl.pallas_call(
        functools.partial(_wo_ln_body, alpha=alpha),
        grid=(M // tm,),
        in_specs=[pl.BlockSpec((tm, D), row), pl.BlockSpec((None, D, D), layer), pl.BlockSpec((tm, D), row),
                  pl.BlockSpec((None, 1, D), layer), pl.BlockSpec((None, 1, D), layer)],
        out_specs=[pl.BlockSpec((tm, D), row), pl.BlockSpec((tm, D), row)],
        out_shape=[jax.ShapeDtypeStruct((M, D), F32), jax.ShapeDtypeStruct((M, D), BF16)],
        compiler_params=_cparams("parallel"), name="wo_ln")(mix, w_o, x32, g, b)


def _mlp_body(x_ref, wu_ref, wd_ref, g_ref, b_ref, o32_ref, o16_ref, xb_ref, *, alpha):
    f = pl.program_id(1)

    @pl.when(f == 0)
    def _():
        xb_ref[...] = x_ref[...].astype(BF16)
        o32_ref[...] = jnp.zeros_like(o32_ref)

    half = x_ref.shape[0] // 2
    for rows in (slice(0, half), slice(half, None)):
        h = jnp.maximum(_dot(xb_ref[rows, :], wu_ref[...]), 0.0)
        o32_ref[rows, :] += _dot((h * h).astype(BF16), wd_ref[...])

    @pl.when(f == pl.num_programs(1) - 1)
    def _():
        out = _layer_norm(alpha * x_ref[...] + o32_ref[...], g_ref[...], b_ref[...])
        o32_ref[...] = out
        o16_ref[...] = out.astype(o16_ref.dtype)


def _mlp(x32, w_up, w_down, l, g, b, alpha, *, tm_pref=512, tf_pref=1024):
    M, D = x32.shape
    DFF = w_up.shape[2]
    tm = _tile(M, tm_pref, 16)
    tf = _tile(DFF, tf_pref, LANES)
    row = lambda i, f: (i, 0)
    layer = lambda i, f: (l, 0, 0)
    return pl.pallas_call(
        functools.partial(_mlp_body, alpha=alpha),
        grid=(M // tm, DFF // tf),
        in_specs=[pl.BlockSpec((tm, D), row), pl.BlockSpec((None, D, tf), lambda i, f: (l, 0, f)),
                  pl.BlockSpec((None, tf, D), lambda i, f: (l, f, 0)),
                  pl.BlockSpec((None, 1, D), layer), pl.BlockSpec((None, 1, D), layer)],
        out_specs=[pl.BlockSpec((tm, D), row), pl.BlockSpec((tm, D), row)],
        out_shape=[jax.ShapeDtypeStruct((M, D), F32), jax.ShapeDtypeStruct((M, D), BF16)],
        scratch_shapes=[pltpu.VMEM((tm, D), BF16)],
        compiler_params=_cparams("parallel", "arbitrary"), name="mlp_ln")(x32, w_up, w_down, g, b)


def _conv_sample_body(s_ref, u_ref, w_ref, bdw_ref, g_ref, b_ref, *refs):
    h_ref, ns_ref = refs[-2:]
    Wm1 = s_ref.shape[0]
    u = u_ref[...]
    acc = u * w_ref[Wm1:Wm1 + 1, :]
    for j in range(Wm1):
        acc = acc + s_ref[j] * w_ref[j:j + 1, :]
    y = _layer_norm(acc + bdw_ref[...], g_ref[...], b_ref[...])
    h_ref[...] = (y * jax.nn.sigmoid(y)).astype(h_ref.dtype)
    ns_ref[0:Wm1 - 1] = s_ref[1:Wm1]
    ns_ref[Wm1 - 1] = u


def _conv_sample(state_t, l, u, w_dw, b_dw, g, b, carried):
    depth, Wm1, Bs, C = state_t.shape
    bt = _tile(Bs, 16, SUBLANES)
    vec = pl.BlockSpec((None, 1, C), lambda i: (l, 0, 0))
    state_spec = pl.BlockSpec((None, Wm1, bt, C), lambda i: (l, 0, i, 0))
    in_specs = [state_spec, pl.BlockSpec((bt, C), lambda i: (i, 0)),
                pl.BlockSpec((None, Wm1 + 1, C), lambda i: (l, 0, 0)), vec, vec, vec]
    args = [state_t, u, w_dw, b_dw, g, b]
    aliases = {}
    if carried is not None:
        aliases = {len(args): 1}
        in_specs.append(pl.BlockSpec(memory_space=pl.ANY))
        args.append(carried)
    return pl.pallas_call(
        _conv_sample_body,
        grid=(Bs // bt,), in_specs=in_specs,
        out_specs=[pl.BlockSpec((bt, C), lambda i: (i, 0)), state_spec],
        out_shape=[jax.ShapeDtypeStruct((Bs, C), BF16), jax.ShapeDtypeStruct((depth, Wm1, Bs, C), F32)],
        input_output_aliases=aliases, compiler_params=_cparams("parallel"), name="conv_sample")(*args)


def _scores3(k3, qs):
    return jnp.sum(k3 * qs[None], axis=-1, keepdims=True)


def _online_update(state, s3, v3):
    m, l, acc = state
    m_new = jnp.maximum(m, jnp.max(s3, axis=0))
    a = jnp.exp(m - m_new)
    e3 = jnp.exp(s3 - m_new[None])
    return m_new, a * l + jnp.sum(e3, axis=0), a * acc + jnp.sum(e3 * v3, axis=0)


def _softmax_state(H, Dh):
    return jnp.full((H, 1), -jnp.inf, F32), jnp.zeros((H, 1), F32), jnp.zeros((H, Dh), F32)


def _fox_sample_body(pt_ref, q_ref, kn_ref, vn_ref, fn_ref, *refs, n_pages, scale):
    del pt_ref
    k_refs, v_refs, f_refs = (refs[i * n_pages:(i + 1) * n_pages] for i in range(3))
    o_ref = refs[3 * n_pages]
    H, Dh = q_ref.shape
    PS = k_refs[0].shape[0]
    assert Dh == LANES
    qs = q_ref[...] * scale
    later = lax.broadcasted_iota(jnp.int32, (PS, H, PS), 2) > lax.broadcasted_iota(jnp.int32, (PS, H, PS), 0)
    ones = jnp.ones((Dh + 2 * PS, LANES), BF16)
    after = jnp.broadcast_to(fn_ref[...], (H, LANES))
    tops, sums, vals = [], [], []
    for p in range(n_pages - 1, -1, -1):
        ft = f_refs[p][...]
        hi = ft.astype(BF16).astype(F32)
        terms = [k_refs[p][...] * qs[None]] + [jnp.where(later, x[None], 0.0) for x in (hi, ft - hi)]
        lhs = jnp.concatenate([t.reshape(PS * H, t.shape[2]).astype(BF16) for t in terms], axis=1)
        raw = _dot(lhs, ones).reshape(PS, H, LANES)
        top = jnp.max(raw, axis=0)
        e3 = jnp.exp(raw - top[None])
        tops.append(top + after)
        sums.append(jnp.sum(e3, axis=0))
        vals.append(jnp.sum(e3 * v_refs[p][...], axis=0))
        after = after + jnp.sum(ft, axis=-1, keepdims=True)
    s_new = jnp.broadcast_to(jnp.sum(kn_ref[...] * qs, axis=-1, keepdims=True), (H, LANES))
    m = functools.reduce(jnp.maximum, tops + [s_new])
    e_new = jnp.exp(s_new - m)
    l = e_new
    acc = e_new * vn_ref[...]
    for top, psum, pval in zip(tops, sums, vals):
        w = jnp.exp(top - m)
        l = l + w * psum
        acc = acc + w * pval
    o_ref[...] = acc / l


def _fox_sample(q, k_new, v_new, logf_new, cache_k, cache_v, cache_logf_t, l, page_table):
    Bs, H, Dh = q.shape
    PS = cache_k.shape[2]
    n_pages = page_table.shape[1]
    one = lambda width: pl.BlockSpec((None, H, width), lambda b, pt: (b, 0, 0))
    kv_page = lambda p: pl.BlockSpec((None, None, PS, H, Dh), lambda b, pt: (l, pt[b, p], 0, 0, 0))
    f_page = lambda p: pl.BlockSpec((None, None, H, PS), lambda b, pt: (l, pt[b, p], 0, 0))
    in_specs = [one(Dh), one(Dh), one(Dh), one(1)]
    in_specs += [kv_page(p) for p in range(n_pages)] + [kv_page(p) for p in range(n_pages)]
    in_specs += [f_page(p) for p in range(n_pages)]
    return pl.pallas_call(
        functools.partial(_fox_sample_body, n_pages=n_pages, scale=Dh ** -0.5),
        grid_spec=pltpu.PrefetchScalarGridSpec(
            num_scalar_prefetch=1, grid=(Bs,), in_specs=in_specs,
            out_specs=pl.BlockSpec((None, H, Dh), lambda b, pt: (b, 0, 0))),
        out_shape=jax.ShapeDtypeStruct((Bs, H, Dh), F32),
        compiler_params=_cparams("arbitrary"), name="fox_sample",
    )(page_table, q, k_new, v_new, logf_new,
      *([cache_k] * n_pages), *([cache_v] * n_pages), *([cache_logf_t] * n_pages))


def _mem_sample_body(q_ref, k_ref, v_ref, o_ref, *, scale):
    G, H, Dh = q_ref.shape
    for g in range(G):
        qs = q_ref[g] * scale
        _, l, acc = _online_update(_softmax_state(H, Dh), _scores3(k_ref[g], qs), v_ref[g])
        o_ref[g] = acc / l


def _mem_sample(q, mem_k, mem_v, l):
    _, Bs, N, H, Dh = mem_k.shape
    G = _tile(Bs, 4, 1)
    blk = lambda i: (i, 0, 0)
    cache = pl.BlockSpec((None, G, N, H, Dh), lambda i: (l, i, 0, 0, 0))
    return pl.pallas_call(
        functools.partial(_mem_sample_body, scale=Dh ** -0.5),
        grid=(Bs // G,),
        in_specs=[pl.BlockSpec((G, H, Dh), blk), cache, cache],
        out_specs=pl.BlockSpec((G, H, Dh), blk),
        out_shape=jax.ShapeDtypeStruct((Bs, H, Dh), F32),
        compiler_params=_cparams("parallel"), name="mem_sample")(q, mem_k, mem_v)


def kernel(x_prompt, x_sample, mem_prompt, cache_k, cache_v, cache_logf, page_table, state_conv, cache_mem_k, cache_mem_v, w_in, b_f, b_gate, w_dw, b_dw, conv_ln_g, conv_ln_b, w_conv_out, w_fox_out, w_mem_kv, w_mem_out, w_o, ln1_g, ln1_b, w_up, w_down, ln2_g, ln2_b):
    depth = w_in.shape[0]
    B, T, D = x_prompt.shape
    Bs, Ts, _ = x_sample.shape
    assert Ts == 1
    W, C = w_dw.shape[1:]
    H, Dh = cache_k.shape[3:]
    DF = H * Dh
    NM, Hm, Dhm = cache_mem_k.shape[2:]
    DM = Hm * Dhm
    alpha = float((2 * depth) ** 0.25)

    c_q, c_k, c_v, c_f = 2 * C, 2 * C + DF, 2 * C + 2 * DF, 2 * C + 3 * DF
    c_tail = c_f + H

    w_main = w_in.astype(BF16)
    w_f = jnp.pad(w_main[:, :, c_f:c_tail], ((0, 0), (0, 0), (0, LANES - H)))
    w_tail = w_main[:, :, c_tail:]
    vec = lambda a: a[:, None, :]
    b_f_pad = vec(jnp.pad(b_f, ((0, 0), (0, LANES - H))))
    w_co, w_fo, w_mo = w_conv_out.astype(BF16), w_fox_out.astype(BF16), w_mem_out.astype(BF16)
    w_mkv, w_o16 = w_mem_kv.astype(BF16), w_o.astype(BF16)
    w_up16, w_down16 = w_up.astype(BF16), w_down.astype(BF16)
    b_gate3, ln1_g3, ln1_b3, ln2_g3, ln2_b3 = vec(b_gate), vec(ln1_g), vec(ln1_b), vec(ln2_g), vec(ln2_b)
    conv_args = (w_dw, vec(b_dw), vec(conv_ln_g), vec(conv_ln_b))

    def project(x16, l, name, kv_carry=None):
        u, = _mm(x16, w_main, l, [0, C], C, [F32], epilogue="glu", name=name + "_glu")
        q16, = _mm(x16, w_main, l, [c_q], DF, [BF16], name=name + "_q")
        per_layer = [None, None] if kv_carry is None else [(depth, c) for c in kv_carry]
        k32, k16 = _mm(x16, w_main, l, [c_k], DF, [F32, BF16], head_dim=Dh, per_layer=per_layer[0], name=name + "_k")
        v32, v16 = _mm(x16, w_main, l, [c_v], DF, [F32, BF16], head_dim=Dh, per_layer=per_layer[1], name=name + "_v")
        logf_pad, = _mm(x16, w_f, l, [0], LANES, [F32], epilogue="logsig", bias=b_f_pad, name=name + "_logf")
        qm16, = _mm(x16, w_tail, l, [0], DM, [BF16], name=name + "_qm")
        return u, q16, k32, k16, v32, v16, logf_pad, qm16

    def merge(x32, x16, branches, l):
        mix = _mix(x16, branches, [w_co, w_fo, w_mo], w_tail, l, DM, b_gate3)
        x1_32, _ = _wo_ln(mix, w_o16, l, x32, ln1_g3, ln1_b3, alpha)
        return _mlp(x1_32, w_up16, w_down16, l, ln2_g3, ln2_b3, alpha)

    xp32 = x_prompt.reshape(B * T, D)
    xp16 = xp32.astype(BF16)
    xs32 = x_sample.reshape(Bs, D)
    xs16 = xs32.astype(BF16)
    mem16 = mem_prompt.reshape(B * NM, D).astype(BF16)
    cache_logf_t = jnp.swapaxes(cache_logf, 2, 3)
    outs = [[] for _ in range(10)]
    state_t = jnp.swapaxes(state_conv, 1, 2)
    k_prompt = v_prompt = None
    conv_state = None
    for l in range(depth):
        u, q16, k_prompt, k16, v_prompt, v16, logf_pad, qm16 = project(xp16, l, "prompt", (k_prompt, v_prompt))
        hc = _conv_prompt(u, l, *conv_args, B, T)
        crow = _cumsum(logf_pad, B, T)
        fa = _fox_prompt(q16, k16, v16, crow, B, T, H, Dh)
        mk32, mk16 = _mm(mem16, w_mkv, l, [0], DM, [F32, BF16], name="mem_k")
        mv32, mv16 = _mm(mem16, w_mkv, l, [DM], DM, [F32, BF16], name="mem_v")
        ma = _memattn_prompt(qm16, mk16, mv16, B, T, Hm, Dhm)
        xp32, xp16 = merge(xp32, xp16, [hc, fa, ma], l)
        outs[2].append(logf_pad[:, :H].reshape(B, T, H))
        outs[3].append(u.reshape(B, T, C)[:, T - (W - 1):])
        outs[4].append(mk32.reshape(B, NM, Hm, Dhm))
        outs[5].append(mv32.reshape(B, NM, Hm, Dhm))

        u, q16, k32, k16, v32, v16, logf_pad, qm16 = project(xs16, l, "sample")
        hc, conv_state = _conv_sample(state_t, l, u, *conv_args, conv_state)
        heads = lambda a: a.astype(F32).reshape(Bs, H, Dh)
        fa = _fox_sample(heads(q16), k32, v32, logf_pad[:, :H, None], cache_k, cache_v, cache_logf_t,
                         l, page_table)
        ma = _mem_sample(qm16.astype(F32).reshape(Bs, Hm, Dhm), cache_mem_k, cache_mem_v, l)
        branches = [hc, fa.reshape(Bs, DF), ma.reshape(Bs, DM)]
        xs32, xs16 = merge(xs32, xs16, [a.astype(BF16) for a in branches], l)
        outs[6].append(k32.reshape(Bs, Ts, H, Dh))
        outs[7].append(v32.reshape(Bs, Ts, H, Dh))
        outs[8].append(logf_pad[:, :H].reshape(Bs, Ts, H))

    stacked = [None if not o else jnp.stack(o) for o in outs]
    stacked[0] = k_prompt.reshape(depth, B, T, H, Dh)
    stacked[1] = v_prompt.reshape(depth, B, T, H, Dh)
    stacked[9] = jnp.swapaxes(conv_state, 1, 2)
    return (xp32.reshape(B, T, D), xs32.reshape(Bs, Ts, D)) + tuple(stacked)
```

```python
import functools

import jax
import jax.numpy as jnp
from jax import lax
from jax.experimental import pallas as pl
from jax.experimental.pallas import tpu as pltpu

F32 = jnp.float32
BF16 = jnp.bfloat16

LN_EPS = 1e-5
LOG2_E = 1.4426950408889634
LANES = 128
SUBLANES = 8
VMEM_LIMIT_BYTES = 56 * 1024 * 1024
CONV_HALO = 32


def _cparams(*sem):
    return pltpu.CompilerParams(dimension_semantics=sem, vmem_limit_bytes=VMEM_LIMIT_BYTES)


def _tile(n, pref, mult):
    t = min(pref, n)
    t -= t % mult
    while t >= mult:
        if n % t == 0:
            return t
        t -= mult
    return n


def _layer_norm(r, g, b):
    mu = jnp.mean(r, axis=-1, keepdims=True)
    d = r - mu
    var = jnp.mean(d * d, axis=-1, keepdims=True)
    return d * lax.rsqrt(var + LN_EPS) * g + b


def _split3(f):
    hi = f.astype(BF16)
    r1 = f - hi.astype(F32)
    mid = r1.astype(BF16)
    lo = (r1 - mid.astype(F32)).astype(BF16)
    return hi, mid, lo


def _dot(a, b):
    return jnp.dot(a, b, preferred_element_type=F32)


def _dot_nt(a, b):
    return lax.dot_general(a, b, (((1,), (1,)), ((), ())), preferred_element_type=F32)


def _mm_body(*refs, n_w, epilogue, has_bias, has_carry):
    x_ref = refs[0]
    w_refs = refs[1:1 + n_w]
    pos = 1 + n_w
    b_ref = refs[pos] if has_bias else None
    o_refs = refs[pos + int(has_bias) + int(has_carry):]
    x = x_ref[...]
    zs = [_dot(x, w[...]) for w in w_refs]
    if epilogue == "glu":
        y = zs[0] * jax.nn.sigmoid(zs[1])
    elif epilogue == "logsig":
        y = jax.nn.log_sigmoid(zs[0] + b_ref[...])
    else:
        y = zs[0]
    for o in o_refs:
        if len(o.shape) == 3:
            Dh = o.shape[2]
            for h in range(o.shape[1]):
                o[:, h, :] = y[:, h * Dh:(h + 1) * Dh].astype(o.dtype)
        else:
            o[...] = y.astype(o.dtype)


def _mm(x, w, l, col_starts, n, out_dtypes, *, epilogue="none", bias=None, head_dim=None, per_layer=None,
        tm_pref=1024, tn_pref=1024, name="mm"):
    M, K = x.shape
    tm = _tile(M, tm_pref, 16)
    tn = n if head_dim else _tile(n, tn_pref, LANES)
    for c in col_starts:
        assert c % tn == 0
    in_specs = [pl.BlockSpec((tm, K), lambda i, j: (i, 0))]
    args = [x]
    for c in col_starts:
        in_specs.append(pl.BlockSpec((None, K, tn), lambda i, j, off=c // tn: (l, 0, j + off)))
        args.append(w)
    if bias is not None:
        in_specs.append(pl.BlockSpec((None, 1, tn), lambda i, j: (l, 0, j)))
        args.append(bias)
    out_shape = [jax.ShapeDtypeStruct((M, n), dt) for dt in out_dtypes]
    out_specs = [pl.BlockSpec((tm, tn), lambda i, j: (i, j)) for _ in out_dtypes]
    aliases = {}
    if head_dim:
        out_shape[0] = jax.ShapeDtypeStruct((M, n // head_dim, head_dim), out_dtypes[0])
        out_specs[0] = pl.BlockSpec((tm, n // head_dim, head_dim), lambda i, j: (i, 0, 0))
    if per_layer is not None:
        depth, carried = per_layer
        out_shape[0] = jax.ShapeDtypeStruct((depth, M, n // head_dim, head_dim), out_dtypes[0])
        out_specs[0] = pl.BlockSpec((None, tm, n // head_dim, head_dim), lambda i, j: (l, i, 0, 0))
        if carried is not None:
            aliases = {len(args): 0}
            in_specs.append(pl.BlockSpec(memory_space=pl.ANY))
            args.append(carried)
    outs = pl.pallas_call(
        functools.partial(_mm_body, n_w=len(col_starts), epilogue=epilogue, has_bias=bias is not None,
                          has_carry=bool(aliases)),
        grid=(M // tm, n // tn), in_specs=in_specs, out_specs=out_specs, out_shape=out_shape,
        input_output_aliases=aliases, compiler_params=_cparams("parallel", "parallel"), name=name)(*args)
    return outs


def _cumsum_body(f_ref, crow_ref, ccol_ref, *, chunk):
    T = f_ref.shape[0]
    r = lax.broadcasted_iota(jnp.int32, (chunk, chunk), 0)
    c = lax.broadcasted_iota(jnp.int32, (chunk, chunk), 1)
    tri = jnp.where(c <= r, 1.0, 0.0).astype(BF16)
    carry = jnp.zeros((1, f_ref.shape[1]), F32)
    for ci in range(T // chunk):
        hi, mid, lo = _split3(f_ref[ci * chunk:(ci + 1) * chunk, :])
        cs = _dot(tri, hi) + _dot(tri, mid) + _dot(tri, lo) + carry
        ccol_ref[ci * chunk:(ci + 1) * chunk, :] = cs
        carry = cs[chunk - 1:chunk, :]
    crow_ref[...] = ccol_ref[...].T


def _cumsum(logf_pad, B, T):
    chunk = _tile(T, 256, LANES)
    return pl.pallas_call(
        functools.partial(_cumsum_body, chunk=chunk),
        grid=(B,),
        in_specs=[pl.BlockSpec((T, LANES), lambda b: (b, 0))],
        out_specs=pl.BlockSpec((None, LANES, T), lambda b: (b, 0, 0)),
        out_shape=jax.ShapeDtypeStruct((B, LANES, T), F32),
        scratch_shapes=[pltpu.VMEM((T, LANES), F32)],
        compiler_params=_cparams("parallel"), name="logf_cumsum")(logf_pad)


def _conv_body(prev_ref, cur_ref, w_ref, bdw_ref, g_ref, b_ref, o_ref, sh_ref, h_ref, *, conv_rows, norm_rows, copy_rows):
    i = pl.program_id(1)
    tt, C = cur_ref.shape
    W = w_ref.shape[0]
    groups = [slice(c * LANES, (c + 1) * LANES) for c in range(C // LANES)]
    n_shifted = tt + CONV_HALO - SUBLANES
    first = CONV_HALO - (W - 1)

    for c, cs in enumerate(groups):
        sh_ref[c, 0, 0:CONV_HALO, :] = jnp.where(i > 0, prev_ref[:, cs], 0.0)
        sh_ref[c, 0, CONV_HALO:, :] = cur_ref[:, cs]
        for s in range(1, SUBLANES):
            for r in range(0, n_shifted, copy_rows):
                sh_ref[c, s, r:r + copy_rows, :] = sh_ref[c, 0, r + s:r + s + copy_rows, :]
        taps = [w_ref[j:j + 1, cs] for j in range(W)]

        def conv_rows_fn(r, carry):
            r0 = pl.multiple_of(r * conv_rows, conv_rows)
            accs = [None, None]
            for j in range(W):
                e = first + j
                t = taps[j] * sh_ref[c, e % SUBLANES, pl.ds(r0 + (e - e % SUBLANES), conv_rows), :]
                accs[j % 2] = t if accs[j % 2] is None else accs[j % 2] + t
            h_ref[c, pl.ds(r0, conv_rows), :] = accs[0] + accs[1]
            return carry

        lax.fori_loop(0, tt // conv_rows, conv_rows_fn, 0)

    def norm_rows_fn(r, carry):
        r0 = pl.multiple_of(r * norm_rows, norm_rows)
        xs = [h_ref[c, pl.ds(r0, norm_rows), :] + bdw_ref[:, cs] for c, cs in enumerate(groups)]
        mu = jnp.sum(functools.reduce(jnp.add, xs), axis=-1, keepdims=True) / C
        ds = [x - mu for x in xs]
        var = jnp.sum(functools.reduce(jnp.add, [d * d for d in ds]), axis=-1, keepdims=True) / C
        inv = lax.rsqrt(var + LN_EPS)
        for d, cs in zip(ds, groups):
            y = d * inv * g_ref[:, cs] + b_ref[:, cs]
            o_ref[pl.ds(r0, norm_rows), cs] = (y * jax.nn.sigmoid(y)).astype(o_ref.dtype)
        return carry

    lax.fori_loop(0, tt // norm_rows, norm_rows_fn, 0)


def _conv_prompt(u, l, w_dw, b_dw, g, b, B, T):
    C = u.shape[1]
    W = w_dw.shape[1]
    vec = pl.BlockSpec((None, 1, C), lambda bi, i: (l, 0, 0))
    assert W - 1 <= CONV_HALO and T % CONV_HALO == 0
    tt = _tile(T, 256, CONV_HALO)
    nt = T // tt
    copy_rows = _tile(tt + CONV_HALO - SUBLANES, 64, SUBLANES)
    return pl.pallas_call(
        functools.partial(_conv_body, conv_rows=_tile(tt, 64, SUBLANES), norm_rows=_tile(tt, 128, 16),
                          copy_rows=copy_rows),
        grid=(B, nt),
        in_specs=[pl.BlockSpec((CONV_HALO, C), lambda bi, i: (jnp.maximum((bi * T + i * tt) // CONV_HALO - 1, 0), 0)),
                  pl.BlockSpec((tt, C), lambda bi, i: (bi * nt + i, 0)),
                  pl.BlockSpec((None, W, C), lambda bi, i: (l, 0, 0)), vec, vec, vec],
        out_specs=pl.BlockSpec((tt, C), lambda bi, i: (bi * nt + i, 0)),
        out_shape=jax.ShapeDtypeStruct((B * T, C), BF16),
        scratch_shapes=[pltpu.VMEM((C // LANES, SUBLANES, CONV_HALO + tt, LANES), F32),
                        pltpu.VMEM((C // LANES, tt, LANES), F32)],
        compiler_params=_cparams("parallel", "parallel"), name="conv_prompt")(u, u, w_dw, b_dw, g, b)


def _fox_body(q_ref, k_ref, v_ref, crow_ref, o_ref, *, H, Dh, scale):
    qi = pl.program_id(1)
    tq = q_ref.shape[0]
    row = lax.broadcasted_iota(jnp.int32, (tq, tq), 0)
    col = lax.broadcasted_iota(jnp.int32, (tq, tq), 1)
    causal = row >= col
    reps = tq // LANES

    def key_block(j, states, masked):
        k0 = pl.multiple_of(j * tq, tq)
        new_states = []
        for h in range(H):
            sl = slice(h * Dh, (h + 1) * Dh)
            m, l, acc = states[h]
            s = (_dot_nt(q_ref[:, sl], k_ref[pl.ds(k0, tq), sl]) * (scale * LOG2_E)
                 - crow_ref[h:h + 1, pl.ds(k0, tq)] * LOG2_E)
            if masked:
                s = jnp.where(causal, s, -jnp.inf)
            m_new = jnp.maximum(m, jnp.broadcast_to(jnp.max(s, axis=-1, keepdims=True), (tq, LANES)))
            a = jnp.exp2(m - m_new)
            p = jnp.exp2(s - jnp.concatenate([m_new] * reps, axis=1))
            l = a * l + functools.reduce(jnp.add, [p[:, c * LANES:(c + 1) * LANES] for c in range(reps)])
            acc = a * acc + _dot(p.astype(BF16), v_ref[pl.ds(k0, tq), sl])
            new_states.append((m_new, l, acc))
        return tuple(new_states)

    init = tuple((jnp.full((tq, LANES), -jnp.inf, F32), jnp.zeros((tq, LANES), F32), jnp.zeros((tq, Dh), F32))
                 for _ in range(H))
    states = lax.fori_loop(0, qi, functools.partial(key_block, masked=False), init)
    states = key_block(qi, states, True)
    for h in range(H):
        _, l, acc = states[h]
        o_ref[:, h * Dh:(h + 1) * Dh] = (acc / jnp.sum(l, axis=-1, keepdims=True)).astype(o_ref.dtype)


def _fox_prompt(q, k, v, crow, B, T, H, Dh):
    DF = H * Dh
    assert Dh == LANES
    tq = _tile(T, 512, LANES)
    nq = T // tq
    return pl.pallas_call(
        functools.partial(_fox_body, H=H, Dh=Dh, scale=Dh ** -0.5),
        grid=(B, nq),
        in_specs=[pl.BlockSpec((tq, DF), lambda b, i: (b * nq + i, 0)),
                  pl.BlockSpec((T, DF), lambda b, i: (b, 0)),
                  pl.BlockSpec((T, DF), lambda b, i: (b, 0)),
                  pl.BlockSpec((None, SUBLANES, T), lambda b, i: (b, 0, 0))],
        out_specs=pl.BlockSpec((tq, DF), lambda b, i: (b * nq + i, 0)),
        out_shape=jax.ShapeDtypeStruct((B * T, DF), BF16),
        compiler_params=_cparams("parallel", "parallel"), name="fox_prompt")(q, k, v, crow)


def _memattn_body(q_ref, k_ref, v_ref, o_ref, *, H, Dh, scale):
    for h in range(H):
        sl = slice(h * Dh, (h + 1) * Dh)
        s = _dot_nt(q_ref[:, sl], k_ref[:, sl]) * scale
        e = jnp.exp(s - jnp.max(s, axis=-1, keepdims=True))
        p = e / jnp.sum(e, axis=-1, keepdims=True)
        o_ref[:, sl] = _dot(p.astype(BF16), v_ref[:, sl]).astype(o_ref.dtype)


def _memattn_prompt(q, mk, mv, B, T, H, Dh):
    DM = H * Dh
    N = mk.shape[0] // B
    tq = _tile(T, 512, LANES)
    nq = T // tq
    return pl.pallas_call(
        functools.partial(_memattn_body, H=H, Dh=Dh, scale=Dh ** -0.5),
        grid=(B, nq),
        in_specs=[pl.BlockSpec((tq, DM), lambda b, i: (b * nq + i, 0)),
                  pl.BlockSpec((N, DM), lambda b, i: (b, 0)),
                  pl.BlockSpec((N, DM), lambda b, i: (b, 0))],
        out_specs=pl.BlockSpec((tq, DM), lambda b, i: (b * nq + i, 0)),
        out_shape=jax.ShapeDtypeStruct((B * T, DM), BF16),
        compiler_params=_cparams("parallel", "parallel"), name="memattn_prompt")(q, mk, mv)


def _mix_body(x_ref, *refs, nb):
    h_refs, w_refs, wg_refs, bg_refs = (refs[i * nb:(i + 1) * nb] for i in range(4))
    o_ref = refs[4 * nb]
    x = x_ref[...]
    mix = None
    for h_ref, w_ref, wg_ref, bg_ref in zip(h_refs, w_refs, wg_refs, bg_refs):
        term = jax.nn.sigmoid(_dot(x, wg_ref[...]) + bg_ref[...]) * _dot(h_ref[...], w_ref[...])
        mix = term if mix is None else mix + term
    o_ref[...] = mix.astype(o_ref.dtype)


def _mix(x16, branches, w_outs, w_tail, l, gate_col0, b_gate, *, tm_pref=1024, tn_pref=512):
    M, D = x16.shape
    nb = len(branches)
    tm = _tile(M, tm_pref, 16)
    tn = _tile(D, tn_pref, LANES)
    assert gate_col0 % tn == 0
    in_specs = [pl.BlockSpec((tm, D), lambda i, j: (i, 0))]
    in_specs += [pl.BlockSpec((tm, h.shape[1]), lambda i, j: (i, 0)) for h in branches]
    in_specs += [pl.BlockSpec((None, w.shape[1], tn), lambda i, j: (l, 0, j)) for w in w_outs]
    in_specs += [pl.BlockSpec((None, D, tn), lambda i, j, off=(gate_col0 + b * D) // tn: (l, 0, j + off))
                 for b in range(nb)]
    in_specs += [pl.BlockSpec((None, 1, tn), lambda i, j, off=(b * D) // tn: (l, 0, j + off)) for b in range(nb)]
    return pl.pallas_call(
        functools.partial(_mix_body, nb=nb),
        grid=(M // tm, D // tn), in_specs=in_specs,
        out_specs=pl.BlockSpec((tm, tn), lambda i, j: (i, j)),
        out_shape=jax.ShapeDtypeStruct((M, D), BF16),
        compiler_params=_cparams("parallel", "parallel"), name="gated_mix",
    )(x16, *branches, *w_outs, *([w_tail] * nb), *([b_gate] * nb))


def _wo_ln_body(mix_ref, w_ref, x_ref, g_ref, b_ref, o32_ref, o16_ref, *, alpha):
    half = x_ref.shape[0] // 2
    for rows in (slice(0, half), slice(half, None)):
        y = _layer_norm(alpha * x_ref[rows, :] + _dot(mix_ref[rows, :], w_ref[...]), g_ref[...], b_ref[...])
        o32_ref[rows, :] = y
        o16_ref[rows, :] = y.astype(o16_ref.dtype)


def _wo_ln(mix, w_o, l, x32, g, b, alpha, *, tm_pref=512):
    M, D = x32.shape
    tm = _tile(M, tm_pref, 16)
    row = lambda i: (i, 0)
    layer = lambda i: (l, 0, 0)
    return pl.pallas_call(
        functools.partial(_wo_ln_body, alpha=alpha),
        grid=(M // tm,),
        in_specs=[pl.BlockSpec((tm, D), row), pl.BlockSpec((None, D, D), layer), pl.BlockSpec((tm, D), row),
                  pl.BlockSpec((None, 1, D), layer), pl.BlockSpec((None, 1, D), layer)],
        out_specs=[pl.BlockSpec((tm, D), row), pl.BlockSpec((tm, D), row)],
        out_shape=[jax.ShapeDtypeStruct((M, D), F32), jax.ShapeDtypeStruct((M, D), BF16)],
        compiler_params=_cparams("parallel"), name="wo_ln")(mix, w_o, x32, g, b)


def _mlp_body(x_ref, wu_ref, wd_ref, g_ref, b_ref, o32_ref, o16_ref, xb_ref, *, alpha):
    f = pl.program_id(1)

    @pl.when(f == 0)
    def _():
        xb_ref[...] = x_ref[...].astype(BF16)
        o32_ref[...] = jnp.zeros_like(o32_ref)

    half = x_ref.shape[0] // 2
    for rows in (slice(0, half), slice(half, None)):
        h = jnp.maximum(_dot(xb_ref[rows, :], wu_ref[...]), 0.0)
        o32_ref[rows, :] += _dot((h * h).astype(BF16), wd_ref[...])

    @pl.when(f == pl.num_programs(1) - 1)
    def _():
        out = _layer_norm(alpha * x_ref[...] + o32_ref[...], g_ref[...], b_ref[...])
        o32_ref[...] = out
        o16_ref[...] = out.astype(o16_ref.dtype)


def _mlp(x32, w_up, w_down, l, g, b, alpha, *, tm_pref=512, tf_pref=1024):
    M, D = x32.shape
    DFF = w_up.shape[2]
    tm = _tile(M, tm_pref, 16)
    tf = _tile(DFF, tf_pref, LANES)
    row = lambda i, f: (i, 0)
    layer = lambda i, f: (l, 0, 0)
    return pl.pallas_call(
        functools.partial(_mlp_body, alpha=alpha),
        grid=(M // tm, DFF // tf),
        in_specs=[pl.BlockSpec((tm, D), row), pl.BlockSpec((None, D, tf), lambda i, f: (l, 0, f)),
                  pl.BlockSpec((None, tf, D), lambda i, f: (l, f, 0)),
                  pl.BlockSpec((None, 1, D), layer), pl.BlockSpec((None, 1, D), layer)],
        out_specs=[pl.BlockSpec((tm, D), row), pl.BlockSpec((tm, D), row)],
        out_shape=[jax.ShapeDtypeStruct((M, D), F32), jax.ShapeDtypeStruct((M, D), BF16)],
        scratch_shapes=[pltpu.VMEM((tm, D), BF16)],
        compiler_params=_cparams("parallel", "arbitrary"), name="mlp_ln")(x32, w_up, w_down, g, b)


def _conv_sample_body(s_ref, u_ref, w_ref, bdw_ref, g_ref, b_ref, *refs):
    h_ref, ns_ref = refs[-2:]
    Wm1 = s_ref.shape[0]
    u = u_ref[...]
    acc = u * w_ref[Wm1:Wm1 + 1, :]
    for j in range(Wm1):
        acc = acc + s_ref[j] * w_ref[j:j + 1, :]
    y = _layer_norm(acc + bdw_ref[...], g_ref[...], b_ref[...])
    h_ref[...] = (y * jax.nn.sigmoid(y)).astype(h_ref.dtype)
    ns_ref[0:Wm1 - 1] = s_ref[1:Wm1]
    ns_ref[Wm1 - 1] = u


def _conv_sample(state_t, l, u, w_dw, b_dw, g, b, carried):
    depth, Wm1, Bs, C = state_t.shape
    bt = _tile(Bs, 16, SUBLANES)
    vec = pl.BlockSpec((None, 1, C), lambda i: (l, 0, 0))
    state_spec = pl.BlockSpec((None, Wm1, bt, C), lambda i: (l, 0, i, 0))
    in_specs = [state_spec, pl.BlockSpec((bt, C), lambda i: (i, 0)),
                pl.BlockSpec((None, Wm1 + 1, C), lambda i: (l, 0, 0)), vec, vec, vec]
    args = [state_t, u, w_dw, b_dw, g, b]
    aliases = {}
    if carried is not None:
        aliases = {len(args): 1}
        in_specs.append(pl.BlockSpec(memory_space=pl.ANY))
        args.append(carried)
    return pl.pallas_call(
        _conv_sample_body,
        grid=(Bs // bt,), in_specs=in_specs,
        out_specs=[pl.BlockSpec((bt, C), lambda i: (i, 0)), state_spec],
        out_shape=[jax.ShapeDtypeStruct((Bs, C), BF16), jax.ShapeDtypeStruct((depth, Wm1, Bs, C), F32)],
        input_output_aliases=aliases, compiler_params=_cparams("parallel"), name="conv_sample")(*args)


def _scores3(k3, qs):
    return jnp.sum(k3 * qs[None], axis=-1, keepdims=True)


def _online_update(state, s3, v3):
    m, l, acc = state
    m_new = jnp.maximum(m, jnp.max(s3, axis=0))
    a = jnp.exp(m - m_new)
    e3 = jnp.exp(s3 - m_new[None])
    return m_new, a * l + jnp.sum(e3, axis=0), a * acc + jnp.sum(e3 * v3, axis=0)


def _softmax_state(H, Dh):
    return jnp.full((H, 1), -jnp.inf, F32), jnp.zeros((H, 1), F32), jnp.zeros((H, Dh), F32)


def _fox_sample_body(pt_ref, q_ref, kn_ref, vn_ref, fn_ref, ck_hbm, cv_hbm, cf_hbm, o_ref, kbuf, vbuf, fbuf, sem,
                     *, layer, n_pages, scale):
    b = pl.program_id(0)
    slot = b % 2
    H, Dh = q_ref.shape
    PS = kbuf.shape[2]
    assert Dh == LANES

    def page_copies(sample, into):
        copies = []
        for p in range(n_pages):
            page = pt_ref[sample, p]
            copies.append(pltpu.make_async_copy(ck_hbm.at[layer, page], kbuf.at[into, p], sem.at[into, 0]))
            copies.append(pltpu.make_async_copy(cv_hbm.at[layer, page], vbuf.at[into, p], sem.at[into, 1]))
            copies.append(pltpu.make_async_copy(cf_hbm.at[layer, page], fbuf.at[into, p], sem.at[into, 2]))
        return copies

    @pl.when(b == 0)
    def _():
        for c in page_copies(0, 0):
            c.start()

    @pl.when(b + 1 < pl.num_programs(0))
    def _():
        for c in page_copies(b + 1, 1 - slot):
            c.start()

    for c in page_copies(b, slot):
        c.wait()
    k_refs = [kbuf.at[slot, p] for p in range(n_pages)]
    v_refs = [vbuf.at[slot, p] for p in range(n_pages)]
    f_refs = [fbuf.at[slot, p] for p in range(n_pages)]
    qs = q_ref[...] * scale
    later = lax.broadcasted_iota(jnp.int32, (PS, H, PS), 2) > lax.broadcasted_iota(jnp.int32, (PS, H, PS), 0)
    ones = jnp.ones((Dh + 2 * PS, LANES), BF16)
    after = jnp.broadcast_to(fn_ref[...], (H, LANES))
    tops, sums, vals = [], [], []
    for p in range(n_pages - 1, -1, -1):
        ft = f_refs[p][...]
        hi = ft.astype(BF16).astype(F32)
        terms = [k_refs[p][...] * qs[None]] + [jnp.where(later, x[None], 0.0) for x in (hi, ft - hi)]
        lhs = jnp.concatenate([t.reshape(PS * H, t.shape[2]).astype(BF16) for t in terms], axis=1)
        raw = _dot(lhs, ones).reshape(PS, H, LANES)
        top = jnp.max(raw, axis=0)
        e3 = jnp.exp(raw - top[None])
        tops.append(top + after)
        sums.append(jnp.sum(e3, axis=0))
        vals.append(jnp.sum(e3 * v_refs[p][...], axis=0))
        after = after + jnp.sum(ft, axis=-1, keepdims=True)
    s_new = jnp.broadcast_to(jnp.sum(kn_ref[...] * qs, axis=-1, keepdims=True), (H, LANES))
    m = functools.reduce(jnp.maximum, tops + [s_new])
    e_new = jnp.exp(s_new - m)
    l = e_new
    acc = e_new * vn_ref[...]
    for top, psum, pval in zip(tops, sums, vals):
        w = jnp.exp(top - m)
        l = l + w * psum
        acc = acc + w * pval
    o_ref[...] = acc / l


def _fox_sample(q, k_new, v_new, logf_new, cache_k, cache_v, cache_logf_t, l, page_table):
    Bs, H, Dh = q.shape
    PS = cache_k.shape[2]
    n_pages = page_table.shape[1]
    one = lambda width: pl.BlockSpec((None, H, width), lambda b, pt: (b, 0, 0))
    in_hbm = pl.BlockSpec(memory_space=pl.ANY)
    return pl.pallas_call(
        functools.partial(_fox_sample_body, layer=l, n_pages=n_pages, scale=Dh ** -0.5),
        grid_spec=pltpu.PrefetchScalarGridSpec(
            num_scalar_prefetch=1, grid=(Bs,),
            in_specs=[one(Dh), one(Dh), one(Dh), one(1), in_hbm, in_hbm, in_hbm],
            out_specs=pl.BlockSpec((None, H, Dh), lambda b, pt: (b, 0, 0)),
            scratch_shapes=[pltpu.VMEM((2, n_pages, PS, H, Dh), F32), pltpu.VMEM((2, n_pages, PS, H, Dh), F32),
                            pltpu.VMEM((2, n_pages, H, PS), F32), pltpu.SemaphoreType.DMA((2, 3))]),
        out_shape=jax.ShapeDtypeStruct((Bs, H, Dh), F32),
        compiler_params=_cparams("arbitrary"), name="fox_sample",
    )(page_table, q, k_new, v_new, logf_new, cache_k, cache_v, cache_logf_t)


def _mem_sample_body(q_ref, k_ref, v_ref, o_ref, *, scale):
    G, H, Dh = q_ref.shape
    for g in range(G):
        qs = q_ref[g] * scale
        _, l, acc = _online_update(_softmax_state(H, Dh), _scores3(k_ref[g], qs), v_ref[g])
        o_ref[g] = acc / l


def _mem_sample(q, mem_k, mem_v, l):
    _, Bs, N, H, Dh = mem_k.shape
    G = _tile(Bs, 4, 1)
    blk = lambda i: (i, 0, 0)
    cache = pl.BlockSpec((None, G, N, H, Dh), lambda i: (l, i, 0, 0, 0))
    return pl.pallas_call(
        functools.partial(_mem_sample_body, scale=Dh ** -0.5),
        grid=(Bs // G,),
        in_specs=[pl.BlockSpec((G, H, Dh), blk), cache, cache],
        out_specs=pl.BlockSpec((G, H, Dh), blk),
        out_shape=jax.ShapeDtypeStruct((Bs, H, Dh), F32),
        compiler_params=_cparams("parallel"), name="mem_sample")(q, mem_k, mem_v)


def kernel(x_prompt, x_sample, mem_prompt, cache_k, cache_v, cache_logf, page_table, state_conv, cache_mem_k, cache_mem_v, w_in, b_f, b_gate, w_dw, b_dw, conv_ln_g, conv_ln_b, w_conv_out, w_fox_out, w_mem_kv, w_mem_out, w_o, ln1_g, ln1_b, w_up, w_down, ln2_g, ln2_b):
    depth = w_in.shape[0]
    B, T, D = x_prompt.shape
    Bs, Ts, _ = x_sample.shape
    assert Ts == 1
    W, C = w_dw.shape[1:]
    H, Dh = cache_k.shape[3:]
    DF = H * Dh
    NM, Hm, Dhm = cache_mem_k.shape[2:]
    DM = Hm * Dhm
    alpha = float((2 * depth) ** 0.25)

    c_q, c_k, c_v, c_f = 2 * C, 2 * C + DF, 2 * C + 2 * DF, 2 * C + 3 * DF
    c_tail = c_f + H

    w_main = w_in.astype(BF16)
    w_f = jnp.pad(w_main[:, :, c_f:c_tail], ((0, 0), (0, 0), (0, LANES - H)))
    w_tail = w_main[:, :, c_tail:]
    vec = lambda a: a[:, None, :]
    b_f_pad = vec(jnp.pad(b_f, ((0, 0), (0, LANES - H))))
    w_co, w_fo, w_mo = w_conv_out.astype(BF16), w_fox_out.astype(BF16), w_mem_out.astype(BF16)
    w_mkv, w_o16 = w_mem_kv.astype(BF16), w_o.astype(BF16)
    w_up16, w_down16 = w_up.astype(BF16), w_down.astype(BF16)
    b_gate3, ln1_g3, ln1_b3, ln2_g3, ln2_b3 = vec(b_gate), vec(ln1_g), vec(ln1_b), vec(ln2_g), vec(ln2_b)
    conv_args = (w_dw, vec(b_dw), vec(conv_ln_g), vec(conv_ln_b))

    def project(x16, l, name, kv_carry=None):
        u, = _mm(x16, w_main, l, [0, C], C, [F32], epilogue="glu", name=name + "_glu")
        q16, = _mm(x16, w_main, l, [c_q], DF, [BF16], name=name + "_q")
        per_layer = [None, None] if kv_carry is None else [(depth, c) for c in kv_carry]
        k32, k16 = _mm(x16, w_main, l, [c_k], DF, [F32, BF16], head_dim=Dh, per_layer=per_layer[0], name=name + "_k")
        v32, v16 = _mm(x16, w_main, l, [c_v], DF, [F32, BF16], head_dim=Dh, per_layer=per_layer[1], name=name + "_v")
        logf_pad, = _mm(x16, w_f, l, [0], LANES, [F32], epilogue="logsig", bias=b_f_pad, name=name + "_logf")
        qm16, = _mm(x16, w_tail, l, [0], DM, [BF16], name=name + "_qm")
        return u, q16, k32, k16, v32, v16, logf_pad, qm16

    def merge(x32, x16, branches, l):
        mix = _mix(x16, branches, [w_co, w_fo, w_mo], w_tail, l, DM, b_gate3)
        x1_32, _ = _wo_ln(mix, w_o16, l, x32, ln1_g3, ln1_b3, alpha)
        return _mlp(x1_32, w_up16, w_down16, l, ln2_g3, ln2_b3, alpha)

    xp32 = x_prompt.reshape(B * T, D)
    xp16 = xp32.astype(BF16)
    xs32 = x_sample.reshape(Bs, D)
    xs16 = xs32.astype(BF16)
    mem16 = mem_prompt.reshape(B * NM, D).astype(BF16)
    cache_logf_t = jnp.swapaxes(cache_logf, 2, 3)
    outs = [[] for _ in range(10)]
    state_t = jnp.swapaxes(state_conv, 1, 2)
    k_prompt = v_prompt = None
    conv_state = None
    for l in range(depth):
        u, q16, k_prompt, k16, v_prompt, v16, logf_pad, qm16 = project(xp16, l, "prompt", (k_prompt, v_prompt))
        hc = _conv_prompt(u, l, *conv_args, B, T)
        crow = _cumsum(logf_pad, B, T)
        fa = _fox_prompt(q16, k16, v16, crow, B, T, H, Dh)
        mk32, mk16 = _mm(mem16, w_mkv, l, [0], DM, [F32, BF16], name="mem_k")
        mv32, mv16 = _mm(mem16, w_mkv, l, [DM], DM, [F32, BF16], name="mem_v")
        ma = _memattn_prompt(qm16, mk16, mv16, B, T, Hm, Dhm)
        xp32, xp16 = merge(xp32, xp16, [hc, fa, ma], l)
        outs[2].append(logf_pad[:, :H].reshape(B, T, H))
        outs[3].append(u.reshape(B, T, C)[:, T - (W - 1):])
        outs[4].append(mk32.reshape(B, NM, Hm, Dhm))
        outs[5].append(mv32.reshape(B, NM, Hm, Dhm))

        u, q16, k32, k16, v32, v16, logf_pad, qm16 = project(xs16, l, "sample")
        hc, conv_state = _conv_sample(state_t, l, u, *conv_args, conv_state)
        heads = lambda a: a.astype(F32).reshape(Bs, H, Dh)
        fa = _fox_sample(heads(q16), k32, v32, logf_pad[:, :H, None], cache_k, cache_v, cache_logf_t,
                         l, page_table)
        ma = _mem_sample(qm16.astype(F32).reshape(Bs, Hm, Dhm), cache_mem_k, cache_mem_v, l)
        branches = [hc, fa.reshape(Bs, DF), ma.reshape(Bs, DM)]
        xs32, xs16 = merge(xs32, xs16, [a.astype(BF16) for a in branches], l)
        outs[6].append(k32.reshape(Bs, Ts, H, Dh))
        outs[7].append(v32.reshape(Bs, Ts, H, Dh))
        outs[8].append(logf_pad[:, :H].reshape(Bs, Ts, H))

    stacked = [None if not o else jnp.stack(o) for o in outs]
    stacked[0] = k_prompt.reshape(depth, B, T, H, Dh)
    stacked[1] = v_prompt.reshape(depth, B, T, H, Dh)
    stacked[9] = jnp.swapaxes(conv_state, 1, 2)
    return (xp32.reshape(B, T, D), xs32.reshape(Bs, Ts, D)) + tuple(stacked)
```

```python
import functools

import jax
import jax.numpy as jnp
from jax import lax
from jax.experimental import pallas as pl
from jax.experimental.pallas import tpu as pltpu

F32 = jnp.float32
BF16 = jnp.bfloat16

LN_EPS = 1e-5
LOG2_E = 1.4426950408889634
LANES = 128
SUBLANES = 8
VMEM_LIMIT_BYTES = 56 * 1024 * 1024
CONV_HALO = 32


def _cparams(*sem):
    return pltpu.CompilerParams(dimension_semantics=sem, vmem_limit_bytes=VMEM_LIMIT_BYTES)


def _tile(n, pref, mult):
    t = min(pref, n)
    t -= t % mult
    while t >= mult:
        if n % t == 0:
            return t
        t -= mult
    return n


def _layer_norm(r, g, b):
    mu = jnp.mean(r, axis=-1, keepdims=True)
    d = r - mu
    var = jnp.mean(d * d, axis=-1, keepdims=True)
    return d * lax.rsqrt(var + LN_EPS) * g + b


def _split3(f):
    hi = f.astype(BF16)
    r1 = f - hi.astype(F32)
    mid = r1.astype(BF16)
    lo = (r1 - mid.astype(F32)).astype(BF16)
    return hi, mid, lo


def _dot(a, b):
    return jnp.dot(a, b, preferred_element_type=F32)


def _dot_nt(a, b):
    return lax.dot_general(a, b, (((1,), (1,)), ((), ())), preferred_element_type=F32)


def _mm_body(*refs, n_w, epilogue, has_bias, has_carry):
    x_ref = refs[0]
    w_refs = refs[1:1 + n_w]
    pos = 1 + n_w
    b_ref = refs[pos] if has_bias else None
    o_refs = refs[pos + int(has_bias) + int(has_carry):]
    x = x_ref[...]
    zs = [_dot(x, w[...]) for w in w_refs]
    if epilogue == "glu":
        y = zs[0] * jax.nn.sigmoid(zs[1])
    elif epilogue == "logsig":
        y = jax.nn.log_sigmoid(zs[0] + b_ref[...])
    else:
        y = zs[0]
    for o in o_refs:
        if len(o.shape) == 3:
            Dh = o.shape[2]
            for h in range(o.shape[1]):
                o[:, h, :] = y[:, h * Dh:(h + 1) * Dh].astype(o.dtype)
        else:
            o[...] = y.astype(o.dtype)


def _mm(x, w, l, col_starts, n, out_dtypes, *, epilogue="none", bias=None, head_dim=None, per_layer=None,
        tm_pref=1024, tn_pref=1024, name="mm"):
    M, K = x.shape
    tm = _tile(M, tm_pref, 16)
    tn = n if head_dim else _tile(n, tn_pref, LANES)
    for c in col_starts:
        assert c % tn == 0
    in_specs = [pl.BlockSpec((tm, K), lambda i, j: (i, 0))]
    args = [x]
    for c in col_starts:
        in_specs.append(pl.BlockSpec((None, K, tn), lambda i, j, off=c // tn: (l, 0, j + off)))
        args.append(w)
    if bias is not None:
        in_specs.append(pl.BlockSpec((None, 1, tn), lambda i, j: (l, 0, j)))
        args.append(bias)
    out_shape = [jax.ShapeDtypeStruct((M, n), dt) for dt in out_dtypes]
    out_specs = [pl.BlockSpec((tm, tn), lambda i, j: (i, j)) for _ in out_dtypes]
    aliases = {}
    if head_dim:
        out_shape[0] = jax.ShapeDtypeStruct((M, n // head_dim, head_dim), out_dtypes[0])
        out_specs[0] = pl.BlockSpec((tm, n // head_dim, head_dim), lambda i, j: (i, 0, 0))
    if per_layer is not None:
        depth, carried = per_layer
        out_shape[0] = jax.ShapeDtypeStruct((depth, M, n // head_dim, head_dim), out_dtypes[0])
        out_specs[0] = pl.BlockSpec((None, tm, n // head_dim, head_dim), lambda i, j: (l, i, 0, 0))
        if carried is not None:
            aliases = {len(args): 0}
            in_specs.append(pl.BlockSpec(memory_space=pl.ANY))
            args.append(carried)
    outs = pl.pallas_call(
        functools.partial(_mm_body, n_w=len(col_starts), epilogue=epilogue, has_bias=bias is not None,
                          has_carry=bool(aliases)),
        grid=(M // tm, n // tn), in_specs=in_specs, out_specs=out_specs, out_shape=out_shape,
        input_output_aliases=aliases, compiler_params=_cparams("parallel", "parallel"), name=name)(*args)
    return outs


def _cumsum_body(f_ref, crow_ref, ccol_ref, *, chunk):
    T = f_ref.shape[0]
    r = lax.broadcasted_iota(jnp.int32, (chunk, chunk), 0)
    c = lax.broadcasted_iota(jnp.int32, (chunk, chunk), 1)
    tri = jnp.where(c <= r, 1.0, 0.0).astype(BF16)
    carry = jnp.zeros((1, f_ref.shape[1]), F32)
    for ci in range(T // chunk):
        hi, mid, lo = _split3(f_ref[ci * chunk:(ci + 1) * chunk, :])
        cs = _dot(tri, hi) + _dot(tri, mid) + _dot(tri, lo) + carry
        ccol_ref[ci * chunk:(ci + 1) * chunk, :] = cs
        carry = cs[chunk - 1:chunk, :]
    crow_ref[...] = ccol_ref[...].T


def _cumsum(logf_pad, B, T):
    chunk = _tile(T, 256, LANES)
    return pl.pallas_call(
        functools.partial(_cumsum_body, chunk=chunk),
        grid=(B,),
        in_specs=[pl.BlockSpec((T, LANES), lambda b: (b, 0))],
        out_specs=pl.BlockSpec((None, LANES, T), lambda b: (b, 0, 0)),
        out_shape=jax.ShapeDtypeStruct((B, LANES, T), F32),
        scratch_shapes=[pltpu.VMEM((T, LANES), F32)],
        compiler_params=_cparams("parallel"), name="logf_cumsum")(logf_pad)


def _conv_body(prev_ref, cur_ref, w_ref, bdw_ref, g_ref, b_ref, o_ref, sh_ref, h_ref, *, conv_rows, norm_rows, copy_rows):
    i = pl.program_id(1)
    tt, C = cur_ref.shape
    W = w_ref.shape[0]
    groups = [slice(c * LANES, (c + 1) * LANES) for c in range(C // LANES)]
    n_shifted = tt + CONV_HALO - SUBLANES
    first = CONV_HALO - (W - 1)

    for c, cs in enumerate(groups):
        sh_ref[c, 0, 0:CONV_HALO, :] = jnp.where(i > 0, prev_ref[:, cs], 0.0)
        sh_ref[c, 0, CONV_HALO:, :] = cur_ref[:, cs]
        for s in range(1, SUBLANES):
            for r in range(0, n_shifted, copy_rows):
                sh_ref[c, s, r:r + copy_rows, :] = sh_ref[c, 0, r + s:r + s + copy_rows, :]
        taps = [w_ref[j:j + 1, cs] for j in range(W)]

        def conv_rows_fn(r, carry):
            r0 = pl.multiple_of(r * conv_rows, conv_rows)
            accs = [None, None]
            for j in range(W):
                e = first + j
                t = taps[j] * sh_ref[c, e % SUBLANES, pl.ds(r0 + (e - e % SUBLANES), conv_rows), :]
                accs[j % 2] = t if accs[j % 2] is None else accs[j % 2] + t
            h_ref[c, pl.ds(r0, conv_rows), :] = accs[0] + accs[1]
            return carry

        lax.fori_loop(0, tt // conv_rows, conv_rows_fn, 0)

    def norm_rows_fn(r, carry):
        r0 = pl.multiple_of(r * norm_rows, norm_rows)
        xs = [h_ref[c, pl.ds(r0, norm_rows), :] + bdw_ref[:, cs] for c, cs in enumerate(groups)]
        mu = jnp.sum(functools.reduce(jnp.add, xs), axis=-1, keepdims=True) / C
        ds = [x - mu for x in xs]
        var = jnp.sum(functools.reduce(jnp.add, [d * d for d in ds]), axis=-1, keepdims=True) / C
        inv = lax.rsqrt(var + LN_EPS)
        for d, cs in zip(ds, groups):
            y = d * inv * g_ref[:, cs] + b_ref[:, cs]
            o_ref[pl.ds(r0, norm_rows), cs] = (y * jax.nn.sigmoid(y)).astype(o_ref.dtype)
        return carry

    lax.fori_loop(0, tt // norm_rows, norm_rows_fn, 0)


def _conv_prompt(u, l, w_dw, b_dw, g, b, B, T):
    C = u.shape[1]
    W = w_dw.shape[1]
    vec = pl.BlockSpec((None, 1, C), lambda bi, i: (l, 0, 0))
    assert W - 1 <= CONV_HALO and T % CONV_HALO == 0
    tt = _tile(T, 256, CONV_HALO)
    nt = T // tt
    copy_rows = _tile(tt + CONV_HALO - SUBLANES, 64, SUBLANES)
    return pl.pallas_call(
        functools.partial(_conv_body, conv_rows=_tile(tt, 64, SUBLANES), norm_rows=_tile(tt, 128, 16),
                          copy_rows=copy_rows),
        grid=(B, nt),
        in_specs=[pl.BlockSpec((CONV_HALO, C), lambda bi, i: (jnp.maximum((bi * T + i * tt) // CONV_HALO - 1, 0), 0)),
                  pl.BlockSpec((tt, C), lambda bi, i: (bi * nt + i, 0)),
                  pl.BlockSpec((None, W, C), lambda bi, i: (l, 0, 0)), vec, vec, vec],
        out_specs=pl.BlockSpec((tt, C), lambda bi, i: (bi * nt + i, 0)),
        out_shape=jax.ShapeDtypeStruct((B * T, C), BF16),
        scratch_shapes=[pltpu.VMEM((C // LANES, SUBLANES, CONV_HALO + tt, LANES), F32),
                        pltpu.VMEM((C // LANES, tt, LANES), F32)],
        compiler_params=_cparams("parallel", "parallel"), name="conv_prompt")(u, u, w_dw, b_dw, g, b)


def _fox_body(q_ref, k_ref, v_ref, crow_ref, o_ref, *, H, Dh, scale):
    qi = pl.program_id(1)
    tq = q_ref.shape[0]
    row = lax.broadcasted_iota(jnp.int32, (tq, tq), 0)
    col = lax.broadcasted_iota(jnp.int32, (tq, tq), 1)
    causal = row >= col
    reps = tq // LANES

    def key_block(j, states, masked):
        k0 = pl.multiple_of(j * tq, tq)
        new_states = []
        for h in range(H):
            sl = slice(h * Dh, (h + 1) * Dh)
            m, l, acc = states[h]
            s = (_dot_nt(q_ref[:, sl], k_ref[pl.ds(k0, tq), sl]) * (scale * LOG2_E)
                 - crow_ref[h:h + 1, pl.ds(k0, tq)] * LOG2_E)
            if masked:
                s = jnp.where(causal, s, -jnp.inf)
            m_new = jnp.maximum(m, jnp.broadcast_to(jnp.max(s, axis=-1, keepdims=True), (tq, LANES)))
            a = jnp.exp2(m - m_new)
            p = jnp.exp2(s - jnp.concatenate([m_new] * reps, axis=1))
            l = a * l + functools.reduce(jnp.add, [p[:, c * LANES:(c + 1) * LANES] for c in range(reps)])
            acc = a * acc + _dot(p.astype(BF16), v_ref[pl.ds(k0, tq), sl])
            new_states.append((m_new, l, acc))
        return tuple(new_states)

    init = tuple((jnp.full((tq, LANES), -jnp.inf, F32), jnp.zeros((tq, LANES), F32), jnp.zeros((tq, Dh), F32))
                 for _ in range(H))
    states = lax.fori_loop(0, qi, functools.partial(key_block, masked=False), init)
    states = key_block(qi, states, True)
    for h in range(H):
        _, l, acc = states[h]
        o_ref[:, h * Dh:(h + 1) * Dh] = (acc / jnp.sum(l, axis=-1, keepdims=True)).astype(o_ref.dtype)


def _fox_prompt(q, k, v, crow, B, T, H, Dh):
    DF = H * Dh
    assert Dh == LANES
    tq = _tile(T, 512, LANES)
    nq = T // tq
    return pl.pallas_call(
        functools.partial(_fox_body, H=H, Dh=Dh, scale=Dh ** -0.5),
        grid=(B, nq),
        in_specs=[pl.BlockSpec((tq, DF), lambda b, i: (b * nq + i, 0)),
                  pl.BlockSpec((T, DF), lambda b, i: (b, 0)),
                  pl.BlockSpec((T, DF), lambda b, i: (b, 0)),
                  pl.BlockSpec((None, SUBLANES, T), lambda b, i: (b, 0, 0))],
        out_specs=pl.BlockSpec((tq, DF), lambda b, i: (b * nq + i, 0)),
        out_shape=jax.ShapeDtypeStruct((B * T, DF), BF16),
        compiler_params=_cparams("parallel", "parallel"), name="fox_prompt")(q, k, v, crow)


def _memattn_body(q_ref, k_ref, v_ref, o_ref, *, H, Dh, scale):
    for h in range(H):
        sl = slice(h * Dh, (h + 1) * Dh)
        s = _dot_nt(q_ref[:, sl], k_ref[:, sl]) * scale
        e = jnp.exp(s - jnp.max(s, axis=-1, keepdims=True))
        p = e / jnp.sum(e, axis=-1, keepdims=True)
        o_ref[:, sl] = _dot(p.astype(BF16), v_ref[:, sl]).astype(o_ref.dtype)


def _memattn_prompt(q, mk, mv, B, T, H, Dh):
    DM = H * Dh
    N = mk.shape[0] // B
    tq = _tile(T, 512, LANES)
    nq = T // tq
    return pl.pallas_call(
        functools.partial(_memattn_body, H=H, Dh=Dh, scale=Dh ** -0.5),
        grid=(B, nq),
        in_specs=[pl.BlockSpec((tq, DM), lambda b, i: (b * nq + i, 0)),
                  pl.BlockSpec((N, DM), lambda b, i: (b, 0)),
                  pl.BlockSpec((N, DM), lambda b, i: (b, 0))],
        out_specs=pl.BlockSpec((tq, DM), lambda b, i: (b * nq + i, 0)),
        out_shape=jax.ShapeDtypeStruct((B * T, DM), BF16),
        compiler_params=_cparams("parallel", "parallel"), name="memattn_prompt")(q, mk, mv)


def _mix_body(x_ref, *refs, nb):
    h_refs, w_refs, wg_refs, bg_refs = (refs[i * nb:(i + 1) * nb] for i in range(4))
    o_ref = refs[4 * nb]
    x = x_ref[...]
    mix = None
    for h_ref, w_ref, wg_ref, bg_ref in zip(h_refs, w_refs, wg_refs, bg_refs):
        term = jax.nn.sigmoid(_dot(x, wg_ref[...]) + bg_ref[...]) * _dot(h_ref[...], w_ref[...])
        mix = term if mix is None else mix + term
    o_ref[...] = mix.astype(o_ref.dtype)


def _mix(x16, branches, w_outs, w_tail, l, gate_col0, b_gate, *, tm_pref=1024, tn_pref=512):
    M, D = x16.shape
    nb = len(branches)
    tm = _tile(M, tm_pref, 16)
    tn = _tile(D, tn_pref, LANES)
    assert gate_col0 % tn == 0
    in_specs = [pl.BlockSpec((tm, D), lambda i, j: (i, 0))]
    in_specs += [pl.BlockSpec((tm, h.shape[1]), lambda i, j: (i, 0)) for h in branches]
    in_specs += [pl.BlockSpec((None, w.shape[1], tn), lambda i, j: (l, 0, j)) for w in w_outs]
    in_specs += [pl.BlockSpec((None, D, tn), lambda i, j, off=(gate_col0 + b * D) // tn: (l, 0, j + off))
                 for b in range(nb)]
    in_specs += [pl.BlockSpec((None, 1, tn), lambda i, j, off=(b * D) // tn: (l, 0, j + off)) for b in range(nb)]
    return pl.pallas_call(
        functools.partial(_mix_body, nb=nb),
        grid=(M // tm, D // tn), in_specs=in_specs,
        out_specs=pl.BlockSpec((tm, tn), lambda i, j: (i, j)),
        out_shape=jax.ShapeDtypeStruct((M, D), BF16),
        compiler_params=_cparams("parallel", "parallel"), name="gated_mix",
    )(x16, *branches, *w_outs, *([w_tail] * nb), *([b_gate] * nb))


def _wo_ln_body(mix_ref, w_ref, x_ref, g_ref, b_ref, o_ref, *, alpha):
    half = x_ref.shape[0] // 2
    for rows in (slice(0, half), slice(half, None)):
        o_ref[rows, :] = _layer_norm(alpha * x_ref[rows, :] + _dot(mix_ref[rows, :], w_ref[...]),
                                     g_ref[...], b_ref[...])


def _wo_ln(mix, w_o, l, x32, g, b, alpha, *, tm_pref=512):
    M, D = x32.shape
    tm = _tile(M, tm_pref, 16)
    row = lambda i: (i, 0)
    layer = lambda i: (l, 0, 0)
    return pl.pallas_call(
        functools.partial(_wo_ln_body, alpha=alpha),
        grid=(M // tm,),
        in_specs=[pl.BlockSpec((tm, D), row), pl.BlockSpec((None, D, D), layer), pl.BlockSpec((tm, D), row),
                  pl.BlockSpec((None, 1, D), layer), pl.BlockSpec((None, 1, D), layer)],
        out_specs=pl.BlockSpec((tm, D), row),
        out_shape=jax.ShapeDtypeStruct((M, D), F32),
        compiler_params=_cparams("parallel"), name="wo_ln")(mix, w_o, x32, g, b)


def _mlp_body(x_ref, wu_ref, wd_ref, g_ref, b_ref, o32_ref, o16_ref, xb_ref, *, alpha):
    f = pl.program_id(1)

    @pl.when(f == 0)
    def _():
        xb_ref[...] = x_ref[...].astype(BF16)
        o32_ref[...] = jnp.zeros_like(o32_ref)

    half = x_ref.shape[0] // 2
    for rows in (slice(0, half), slice(half, None)):
        h = jnp.maximum(_dot(xb_ref[rows, :], wu_ref[...]), 0.0)
        o32_ref[rows, :] += _dot((h * h).astype(BF16), wd_ref[...])

    @pl.when(f == pl.num_programs(1) - 1)
    def _():
        out = _layer_norm(alpha * x_ref[...] + o32_ref[...], g_ref[...], b_ref[...])
        o32_ref[...] = out
        o16_ref[...] = out.astype(o16_ref.dtype)


def _mlp(x32, w_up, w_down, l, g, b, alpha, *, tm_pref=512, tf_pref=1024):
    M, D = x32.shape
    DFF = w_up.shape[2]
    tm = _tile(M, tm_pref, 16)
    tf = _tile(DFF, tf_pref, LANES)
    row = lambda i, f: (i, 0)
    layer = lambda i, f: (l, 0, 0)
    return pl.pallas_call(
        functools.partial(_mlp_body, alpha=alpha),
        grid=(M // tm, DFF // tf),
        in_specs=[pl.BlockSpec((tm, D), row), pl.BlockSpec((None, D, tf), lambda i, f: (l, 0, f)),
                  pl.BlockSpec((None, tf, D), lambda i, f: (l, f, 0)),
                  pl.BlockSpec((None, 1, D), layer), pl.BlockSpec((None, 1, D), layer)],
        out_specs=[pl.BlockSpec((tm, D), row), pl.BlockSpec((tm, D), row)],
        out_shape=[jax.ShapeDtypeStruct((M, D), F32), jax.ShapeDtypeStruct((M, D), BF16)],
        scratch_shapes=[pltpu.VMEM((tm, D), BF16)],
        compiler_params=_cparams("parallel", "arbitrary"), name="mlp_ln")(x32, w_up, w_down, g, b)


def _conv_sample_body(s_ref, u_ref, w_ref, bdw_ref, g_ref, b_ref, *refs):
    h_ref, ns_ref = refs[-2:]
    Wm1 = s_ref.shape[0]
    u = u_ref[...]
    acc = u * w_ref[Wm1:Wm1 + 1, :]
    for j in range(Wm1):
        acc = acc + s_ref[j] * w_ref[j:j + 1, :]
    y = _layer_norm(acc + bdw_ref[...], g_ref[...], b_ref[...])
    h_ref[...] = (y * jax.nn.sigmoid(y)).astype(h_ref.dtype)
    ns_ref[0:Wm1 - 1] = s_ref[1:Wm1]
    ns_ref[Wm1 - 1] = u


def _conv_sample(state_t, l, u, w_dw, b_dw, g, b, carried):
    depth, Wm1, Bs, C = state_t.shape
    bt = _tile(Bs, 16, SUBLANES)
    vec = pl.BlockSpec((None, 1, C), lambda i: (l, 0, 0))
    state_spec = pl.BlockSpec((None, Wm1, bt, C), lambda i: (l, 0, i, 0))
    in_specs = [state_spec, pl.BlockSpec((bt, C), lambda i: (i, 0)),
                pl.BlockSpec((None, Wm1 + 1, C), lambda i: (l, 0, 0)), vec, vec, vec]
    args = [state_t, u, w_dw, b_dw, g, b]
    aliases = {}
    if carried is not None:
        aliases = {len(args): 1}
        in_specs.append(pl.BlockSpec(memory_space=pl.ANY))
        args.append(carried)
    return pl.pallas_call(
        _conv_sample_body,
        grid=(Bs // bt,), in_specs=in_specs,
        out_specs=[pl.BlockSpec((bt, C), lambda i: (i, 0)), state_spec],
        out_shape=[jax.ShapeDtypeStruct((Bs, C), BF16), jax.ShapeDtypeStruct((depth, Wm1, Bs, C), F32)],
        input_output_aliases=aliases, compiler_params=_cparams("parallel"), name="conv_sample")(*args)


def _scores3(k3, qs):
    return jnp.sum(k3 * qs[None], axis=-1, keepdims=True)


def _online_update(state, s3, v3):
    m, l, acc = state
    m_new = jnp.maximum(m, jnp.max(s3, axis=0))
    a = jnp.exp(m - m_new)
    e3 = jnp.exp(s3 - m_new[None])
    return m_new, a * l + jnp.sum(e3, axis=0), a * acc + jnp.sum(e3 * v3, axis=0)


def _softmax_state(H, Dh):
    return jnp.full((H, 1), -jnp.inf, F32), jnp.zeros((H, 1), F32), jnp.zeros((H, Dh), F32)


def _fox_sample_body(pt_ref, q_ref, kn_ref, vn_ref, fn_ref, ck_hbm, cv_hbm, cf_hbm, o_ref, kbuf, vbuf, fbuf, sem,
                     *, layer, n_pages, scale):
    b = pl.program_id(0)
    slot = b % 2
    H, Dh = q_ref.shape
    PS = kbuf.shape[2]
    assert Dh == LANES

    def page_copies(sample, into):
        copies = []
        for p in range(n_pages):
            page = pt_ref[sample, p]
            copies.append(pltpu.make_async_copy(ck_hbm.at[layer, page], kbuf.at[into, p], sem.at[into, 0]))
            copies.append(pltpu.make_async_copy(cv_hbm.at[layer, page], vbuf.at[into, p], sem.at[into, 1]))
            copies.append(pltpu.make_async_copy(cf_hbm.at[layer, page], fbuf.at[into, p], sem.at[into, 2]))
        return copies

    @pl.when(b == 0)
    def _():
        for c in page_copies(0, 0):
            c.start()

    @pl.when(b + 1 < pl.num_programs(0))
    def _():
        for c in page_copies(b + 1, 1 - slot):
            c.start()

    for c in page_copies(b, slot):
        c.wait()
    k_refs = [kbuf.at[slot, p] for p in range(n_pages)]
    v_refs = [vbuf.at[slot, p] for p in range(n_pages)]
    f_refs = [fbuf.at[slot, p] for p in range(n_pages)]
    qs = q_ref[...] * scale
    later = lax.broadcasted_iota(jnp.int32, (PS, H, PS), 2) > lax.broadcasted_iota(jnp.int32, (PS, H, PS), 0)
    ones = jnp.ones((Dh + 2 * PS, LANES), BF16)
    after = jnp.broadcast_to(fn_ref[...], (H, LANES))
    tops, sums, vals = [], [], []
    for p in range(n_pages - 1, -1, -1):
        ft = f_refs[p][...]
        hi = ft.astype(BF16).astype(F32)
        terms = [k_refs[p][...] * qs[None]] + [jnp.where(later, x[None], 0.0) for x in (hi, ft - hi)]
        lhs = jnp.concatenate([t.reshape(PS * H, t.shape[2]).astype(BF16) for t in terms], axis=1)
        raw = _dot(lhs, ones).reshape(PS, H, LANES)
        top = jnp.max(raw, axis=0)
        e3 = jnp.exp(raw - top[None])
        tops.append(top + after)
        sums.append(jnp.sum(e3, axis=0))
        vals.append(jnp.sum(e3 * v_refs[p][...], axis=0))
        after = after + jnp.sum(ft, axis=-1, keepdims=True)
    s_new = jnp.broadcast_to(jnp.sum(kn_ref[...] * qs, axis=-1, keepdims=True), (H, LANES))
    m = functools.reduce(jnp.maximum, tops + [s_new])
    e_new = jnp.exp(s_new - m)
    l = e_new
    acc = e_new * vn_ref[...]
    for top, psum, pval in zip(tops, sums, vals):
        w = jnp.exp(top - m)
        l = l + w * psum
        acc = acc + w * pval
    o_ref[...] = acc / l


def _fox_sample(q, k_new, v_new, logf_new, cache_k, cache_v, cache_logf_t, l, page_table):
    Bs, H, Dh = q.shape
    PS = cache_k.shape[2]
    n_pages = page_table.shape[1]
    one = lambda width: pl.BlockSpec((None, H, width), lambda b, pt: (b, 0, 0))
    in_hbm = pl.BlockSpec(memory_space=pl.ANY)
    return pl.pallas_call(
        functools.partial(_fox_sample_body, layer=l, n_pages=n_pages, scale=Dh ** -0.5),
        grid_spec=pltpu.PrefetchScalarGridSpec(
            num_scalar_prefetch=1, grid=(Bs,),
            in_specs=[one(Dh), one(Dh), one(Dh), one(1), in_hbm, in_hbm, in_hbm],
            out_specs=pl.BlockSpec((None, H, Dh), lambda b, pt: (b, 0, 0)),
            scratch_shapes=[pltpu.VMEM((2, n_pages, PS, H, Dh), F32), pltpu.VMEM((2, n_pages, PS, H, Dh), F32),
                            pltpu.VMEM((2, n_pages, H, PS), F32), pltpu.SemaphoreType.DMA((2, 3))]),
        out_shape=jax.ShapeDtypeStruct((Bs, H, Dh), F32),
        compiler_params=_cparams("arbitrary"), name="fox_sample",
    )(page_table, q, k_new, v_new, logf_new, cache_k, cache_v, cache_logf_t)


def _mem_sample_body(q_ref, k_ref, v_ref, o_ref, *, scale):
    G, H, Dh = q_ref.shape
    for g in range(G):
        qs = q_ref[g] * scale
        _, l, acc = _online_update(_softmax_state(H, Dh), _scores3(k_ref[g], qs), v_ref[g])
        o_ref[g] = acc / l


def _mem_sample(q, mem_k, mem_v, l):
    _, Bs, N, H, Dh = mem_k.shape
    G = _tile(Bs, 4, 1)
    blk = lambda i: (i, 0, 0)
    cache = pl.BlockSpec((None, G, N, H, Dh), lambda i: (l, i, 0, 0, 0))
    return pl.pallas_call(
        functools.partial(_mem_sample_body, scale=Dh ** -0.5),
        grid=(Bs // G,),
        in_specs=[pl.BlockSpec((G, H, Dh), blk), cache, cache],
        out_specs=pl.BlockSpec((G, H, Dh), blk),
        out_shape=jax.ShapeDtypeStruct((Bs, H, Dh), F32),
        compiler_params=_cparams("parallel"), name="mem_sample")(q, mem_k, mem_v)


def kernel(x_prompt, x_sample, mem_prompt, cache_k, cache_v, cache_logf, page_table, state_conv, cache_mem_k, cache_mem_v, w_in, b_f, b_gate, w_dw, b_dw, conv_ln_g, conv_ln_b, w_conv_out, w_fox_out, w_mem_kv, w_mem_out, w_o, ln1_g, ln1_b, w_up, w_down, ln2_g, ln2_b):
    depth = w_in.shape[0]
    B, T, D = x_prompt.shape
    Bs, Ts, _ = x_sample.shape
    assert Ts == 1
    W, C = w_dw.shape[1:]
    H, Dh = cache_k.shape[3:]
    DF = H * Dh
    NM, Hm, Dhm = cache_mem_k.shape[2:]
    DM = Hm * Dhm
    alpha = float((2 * depth) ** 0.25)

    c_q, c_k, c_v, c_f = 2 * C, 2 * C + DF, 2 * C + 2 * DF, 2 * C + 3 * DF
    c_tail = c_f + H

    w_main = w_in.astype(BF16)
    w_f = jnp.pad(w_main[:, :, c_f:c_tail], ((0, 0), (0, 0), (0, LANES - H)))
    w_tail = w_main[:, :, c_tail:]
    vec = lambda a: a[:, None, :]
    b_f_pad = vec(jnp.pad(b_f, ((0, 0), (0, LANES - H))))
    w_co, w_fo, w_mo = w_conv_out.astype(BF16), w_fox_out.astype(BF16), w_mem_out.astype(BF16)
    w_mkv, w_o16 = w_mem_kv.astype(BF16), w_o.astype(BF16)
    w_up16, w_down16 = w_up.astype(BF16), w_down.astype(BF16)
    b_gate3, ln1_g3, ln1_b3, ln2_g3, ln2_b3 = vec(b_gate), vec(ln1_g), vec(ln1_b), vec(ln2_g), vec(ln2_b)
    conv_args = (w_dw, vec(b_dw), vec(conv_ln_g), vec(conv_ln_b))

    def project(x16, l, name, kv_carry=None):
        u, = _mm(x16, w_main, l, [0, C], C, [F32], epilogue="glu", name=name + "_glu")
        q16, = _mm(x16, w_main, l, [c_q], DF, [BF16], name=name + "_q")
        per_layer = [None, None] if kv_carry is None else [(depth, c) for c in kv_carry]
        k32, k16 = _mm(x16, w_main, l, [c_k], DF, [F32, BF16], head_dim=Dh, per_layer=per_layer[0], name=name + "_k")
        v32, v16 = _mm(x16, w_main, l, [c_v], DF, [F32, BF16], head_dim=Dh, per_layer=per_layer[1], name=name + "_v")
        logf_pad, = _mm(x16, w_f, l, [0], LANES, [F32], epilogue="logsig", bias=b_f_pad, name=name + "_logf")
        qm16, = _mm(x16, w_tail, l, [0], DM, [BF16], name=name + "_qm")
        return u, q16, k32, k16, v32, v16, logf_pad, qm16

    def merge(x32, x16, branches, l):
        mix = _mix(x16, branches, [w_co, w_fo, w_mo], w_tail, l, DM, b_gate3)
        x1 = _wo_ln(mix, w_o16, l, x32, ln1_g3, ln1_b3, alpha)
        return _mlp(x1, w_up16, w_down16, l, ln2_g3, ln2_b3, alpha)

    xp32 = x_prompt.reshape(B * T, D)
    xp16 = xp32.astype(BF16)
    xs32 = x_sample.reshape(Bs, D)
    xs16 = xs32.astype(BF16)
    mem16 = mem_prompt.reshape(B * NM, D).astype(BF16)
    cache_logf_t = jnp.swapaxes(cache_logf, 2, 3)
    outs = [[] for _ in range(10)]
    state_t = jnp.swapaxes(state_conv, 1, 2)
    k_prompt = v_prompt = None
    conv_state = None
    for l in range(depth):
        u, q16, k_prompt, k16, v_prompt, v16, logf_pad, qm16 = project(xp16, l, "prompt", (k_prompt, v_prompt))
        hc = _conv_prompt(u, l, *conv_args, B, T)
        crow = _cumsum(logf_pad, B, T)
        fa = _fox_prompt(q16, k16, v16, crow, B, T, H, Dh)
        mk32, mk16 = _mm(mem16, w_mkv, l, [0], DM, [F32, BF16], name="mem_k")
        mv32, mv16 = _mm(mem16, w_mkv, l, [DM], DM, [F32, BF16], name="mem_v")
        ma = _memattn_prompt(qm16, mk16, mv16, B, T, Hm, Dhm)
        xp32, xp16 = merge(xp32, xp16, [hc, fa, ma], l)
        outs[2].append(logf_pad[:, :H].reshape(B, T, H))
        outs[3].append(u.reshape(B, T, C)[:, T - (W - 1):])
        outs[4].append(mk32.reshape(B, NM, Hm, Dhm))
        outs[5].append(mv32.reshape(B, NM, Hm, Dhm))

        u, q16, k32, k16, v32, v16, logf_pad, qm16 = project(xs16, l, "sample")
        hc, conv_state = _conv_sample(state_t, l, u, *conv_args, conv_state)
        heads = lambda a: a.astype(F32).reshape(Bs, H, Dh)
        fa = _fox_sample(heads(q16), k32, v32, logf_pad[:, :H, None], cache_k, cache_v, cache_logf_t,
                         l, page_table)
        ma = _mem_sample(qm16.astype(F32).reshape(Bs, Hm, Dhm), cache_mem_k, cache_mem_v, l)
        branches = [hc, fa.reshape(Bs, DF), ma.reshape(Bs, DM)]
        xs32, xs16 = merge(xs32, xs16, [a.astype(BF16) for a in branches], l)
        outs[6].append(k32.reshape(Bs, Ts, H, Dh))
        outs[7].append(v32.reshape(Bs, Ts, H, Dh))
        outs[8].append(logf_pad[:, :H].reshape(Bs, Ts, H))

    stacked = [None if not o else jnp.stack(o) for o in outs]
    stacked[0] = k_prompt.reshape(depth, B, T, H, Dh)
    stacked[1] = v_prompt.reshape(depth, B, T, H, Dh)
    stacked[9] = jnp.swapaxes(conv_state, 1, 2)
    return (xp32.reshape(B, T, D), xs32.reshape(Bs, Ts, D)) + tuple(stacked)
```
